```python
import jax
import jax.numpy as jnp
from jax import lax
import numpy as np

D_MODEL = 1024
BATCH = 16
SEQ = 256
DEPTH = 2
DEC_BATCH = 8
DEC_SEQ = 1024
PAST_LEN = 256

GRID_W = 64
N_MIXERS = 2
N_MLSTM_LAYERS = (DEPTH + 1) // 2
N_ATTN_LAYERS = DEPTH // 2
EPS = 1e-6

M_HEADS = 4
M_DQK = D_MODEL // (2 * M_HEADS)
M_DV = D_MODEL // M_HEADS
M_CHUNK = 128
M_CONV_W = 3
M_QK_DIM = 2 * M_HEADS * M_DQK
M_V_DIM = M_HEADS * M_DV
M_IN_DIM = M_QK_DIM + M_V_DIM + M_V_DIM + 4 * M_HEADS
M_F_BIAS_LO = 3.0
M_F_BIAS_HI = 6.0

A_HEADS = 16
A_KV_HEADS = 4
A_HEAD_DIM = D_MODEL // A_HEADS
A_ROPE_AXIS = A_HEAD_DIM // 2
ROPE_THETA = 10000.0
Q_BLOCK = 128
A_QKV_DIM = (A_HEADS + 2 * A_KV_HEADS) * A_HEAD_DIM

P_HEADS = 8
P_NKEYS = 128
P_NEXPERTS = P_NKEYS * P_NKEYS
P_DKEY = 256
P_TOPK = 16
P_TOKEN_BLOCK = 128

kernel_name = 'hybrid_mlstm_gqa_peer_diffusion_step'


def rms_norm(x, g):
    xf = x.astype(jnp.float32)
    y = xf * lax.rsqrt(jnp.mean(xf * xf, axis=-1, keepdims=True) + EPS)
    return (y * g.astype(jnp.float32)).astype(x.dtype)


def ada_mod(cond, w, b):
    m = jax.nn.silu(cond) @ w + b
    return jnp.split(m[..., None, :], 6, axis=-1)


def modulate(x, g, shift, scale):
    return rms_norm(x, g) * (1 + scale) + shift


def short_conv(x, w, b):
    c = x.shape[-1]
    y = lax.conv_general_dilated(x, w[:, None, :].astype(x.dtype), window_strides=(1,),
                                 padding=[((M_CONV_W - 1) // 2, M_CONV_W // 2)],
                                 dimension_numbers=('NWC', 'WIO', 'NWC'), feature_group_count=c)
    return jax.nn.silu(y + b)


def mlstm_project(h, w_in, conv_w, conv_b, gate_b):
    B, S, _ = h.shape
    p = h @ w_in
    qk, v, o, gates = jnp.split(p, [M_QK_DIM, M_QK_DIM + M_V_DIM, M_QK_DIM + 2 * M_V_DIM], axis=-1)
    q, k = jnp.split(short_conv(qk, conv_w, conv_b), 2, axis=-1)
    heads = lambda t, d: t.reshape(B, S, M_HEADS, d).transpose(0, 2, 1, 3).astype(jnp.float32)
    q = heads(q, M_DQK)
    k = heads(k, M_DQK) * (M_DQK ** -0.5)
    v = heads(v, M_DV)
    g = (gates + gate_b).astype(jnp.float32).reshape(B, S, 4, M_HEADS).transpose(2, 0, 3, 1)
    gf = (g[0], jax.nn.log_sigmoid(g[1]))
    gb = (g[2], jax.nn.log_sigmoid(g[3]))
    return q, k, v, o, gf, gb


def mlstm_chunk_scan(q, k, v, ig, lf, C0, n0, m0):
    B, NH, S, _ = q.shape
    nc = S // M_CHUNK
    chunks = lambda t: jnp.moveaxis(t.reshape(B, NH, nc, M_CHUNK, *t.shape[3:]), 2, 0)
    lower = jnp.tril(jnp.ones((M_CHUNK, M_CHUNK), dtype=bool))

    def step(carry, xs):
        C, n, m = carry
        qc, kc, vc, ic, fc = xs
        b = jnp.cumsum(fc, axis=-1)
        d = jnp.where(lower, b[..., :, None] - b[..., None, :] + ic[..., None, :], -jnp.inf)
        inter = b + m[..., None]
        m_t = jnp.maximum(inter, jnp.max(d, axis=-1))
        w = jnp.exp(d - m_t[..., None]) * jnp.einsum('bhtd,bhsd->bhts', qc, kc)
        a = jnp.exp(inter - m_t)
        num = a[..., None] * jnp.einsum('bhtd,bhdv->bhtv', qc, C) + jnp.einsum('bhts,bhsv->bhtv', w, vc)
        den = a * jnp.einsum('bhtd,bhd->bht', qc, n) + jnp.sum(w, axis=-1)
        h = num / jnp.maximum(jnp.abs(den), jnp.exp(-m_t))[..., None]
        b_end = b[..., -1]
        to_end = b_end[..., None] - b + ic
        m_new = jnp.maximum(b_end + m, jnp.max(to_end, axis=-1))
        decay = jnp.exp(b_end + m - m_new)
        we = jnp.exp(to_end - m_new[..., None])
        C_new = decay[..., None, None] * C + jnp.einsum('bhs,bhsd,bhsv->bhdv', we, kc, vc)
        n_new = decay[..., None] * n + jnp.einsum('bhs,bhsd->bhd', we, kc)
        return (C_new, n_new, m_new), h

    state, h = lax.scan(step, (C0, n0, m0), (chunks(q), chunks(k), chunks(v), chunks(ig), chunks(lf)))
    h = jnp.moveaxis(h, 0, 2).reshape(B, NH, S, -1)
    return h, state


def mlstm_bidirectional(q, k, v, gf, gb, init_f, init_b):
    flip = lambda t: jnp.flip(t, axis=2)
    hf, sf = mlstm_chunk_scan(q, k, v, gf[0], gf[1], *init_f)
    hb, sb = mlstm_chunk_scan(flip(q), flip(k), flip(v), flip(gb[0]), flip(gb[1]), *init_b)
    return hf + flip(hb), sf, sb


def mlstm_output(h, o, head_g, w_out):
    B, NH, S, DV = h.shape
    hn = h * lax.rsqrt(jnp.mean(h * h, axis=-1, keepdims=True) + EPS)
    hn = hn.transpose(0, 2, 1, 3).reshape(B, S, NH * DV).astype(o.dtype) * head_g
    return (hn * jax.nn.sigmoid(o)) @ w_out


def attn_project(h, w_qkv, q_g, k_g):
    B, S, _ = h.shape
    q, k, v = jnp.split(h @ w_qkv, [A_HEADS * A_HEAD_DIM, (A_HEADS + A_KV_HEADS) * A_HEAD_DIM], axis=-1)
    q = rms_norm(q.reshape(B, S, A_HEADS, A_HEAD_DIM), q_g).transpose(0, 2, 1, 3)
    k = rms_norm(k.reshape(B, S, A_KV_HEADS, A_HEAD_DIM), k_g).transpose(0, 2, 1, 3)
    v = v.reshape(B, S, A_KV_HEADS, A_HEAD_DIM).transpose(0, 2, 1, 3)
    return q, k, v


def axial_rope_tables(n_rows):
    rows = jnp.repeat(jnp.arange(n_rows, dtype=jnp.float32), GRID_W)
    cols = jnp.tile(jnp.arange(GRID_W, dtype=jnp.float32), n_rows)
    half = A_ROPE_AXIS // 2
    inv_freq = ROPE_THETA ** (-jnp.arange(half, dtype=jnp.float32) / half)
    ang = jnp.concatenate([rows[:, None] * inv_freq, cols[:, None] * inv_freq], axis=-1)
    return jnp.cos(ang), jnp.sin(ang)


def apply_axial_rope(x, cos, sin):
    B, H, S, HD = x.shape
    half = A_ROPE_AXIS // 2
    xa = x.reshape(B, H, S, 2, 2, half)
    x1, x2 = xa[..., 0, :], xa[..., 1, :]
    c = cos.reshape(S, 2, half).astype(x.dtype)
    s = sin.reshape(S, 2, half).astype(x.dtype)
    out = jnp.stack([x1 * c - x2 * s, x2 * c + x1 * s], axis=-2)
    return out.reshape(B, H, S, HD)


def block_attention(q, k, v):
    B, NQ, Sq, HD = q.shape
    G = NQ // A_KV_HEADS
    nb = Sq // Q_BLOCK
    qb = jnp.moveaxis(q.reshape(B, A_KV_HEADS, G, nb, Q_BLOCK, HD), 3, 0)
    scale = HD ** -0.5

    def one_block(qblk):
        s = jnp.einsum('bkgqd,bksd->bkgqs', qblk, k, preferred_element_type=jnp.float32) * scale
        p = jax.nn.softmax(s, axis=-1).astype(v.dtype)
        return jnp.einsum('bkgqs,bksd->bkgqd', p, v)

    o = jnp.moveaxis(lax.map(one_block, qb), 0, 3).reshape(B, NQ, Sq, HD)
    return o.transpose(0, 2, 1, 3).reshape(B, Sq, NQ * HD)


def peer_ffn(h, w_q, subkeys, u, v):
    B, S, D = h.shape
    xb_all = h.reshape((B * S) // P_TOKEN_BLOCK, P_TOKEN_BLOCK, D)

    def one_block(xb):
        q = (xb @ w_q).reshape(P_TOKEN_BLOCK, P_HEADS, 2, P_DKEY // 2)
        s = jnp.einsum('tphd,phnd->tphn', q, subkeys, preferred_element_type=jnp.float32)
        s1, i1 = lax.top_k(s[:, :, 0], P_TOPK)
        s2, i2 = lax.top_k(s[:, :, 1], P_TOPK)
        cand = (s1[..., :, None] + s2[..., None, :]).reshape(P_TOKEN_BLOCK, P_HEADS, P_TOPK * P_TOPK)
        cidx = (i1[..., :, None] * P_NKEYS + i2[..., None, :]).reshape(P_TOKEN_BLOCK, P_HEADS, P_TOPK * P_TOPK)
        top, pos = lax.top_k(cand, P_TOPK)
        eidx = jnp.take_along_axis(cidx, pos, axis=-1)
        g = jax.nn.softmax(top, axis=-1).astype(xb.dtype)
        act = jax.nn.gelu(jnp.einsum('tped,td->tpe', u[eidx], xb), approximate=False)
        return jnp.einsum('tpe,tped->td', g * act, v[eidx])

    return lax.map(one_block, xb_all).reshape(B, S, D)


def setup_inputs(seed: int = 0) -> dict:
    key = jax.random.key(seed)
    ks = jax.random.split(key, 28)
    nrm = lambda k, shape, s: jax.random.normal(k, shape, jnp.float32) * s
    f_bias = jnp.linspace(M_F_BIAS_LO, M_F_BIAS_HI, M_HEADS, dtype=jnp.float32)
    zeros_h = jnp.zeros((M_HEADS,), jnp.float32)
    gate_base = jnp.concatenate([zeros_h, f_bias, zeros_h, f_bias])
    return {
        'x_prompt': nrm(ks[0], (BATCH, SEQ, D_MODEL), 1.0),
        'x_sample': nrm(ks[1], (DEC_BATCH, DEC_SEQ, D_MODEL), 1.0),
        'state_mlstm_C': nrm(ks[2], (DEC_BATCH, N_MLSTM_LAYERS, 2, M_HEADS, M_DQK, M_DV), 0.1),
        'state_mlstm_n': nrm(ks[3], (DEC_BATCH, N_MLSTM_LAYERS, 2, M_HEADS, M_DQK), 0.1),
        'state_mlstm_m': nrm(ks[4], (DEC_BATCH, N_MLSTM_LAYERS, 2, M_HEADS), 1.0),
        'cache_attn_k': nrm(ks[5], (DEC_BATCH, N_ATTN_LAYERS, A_KV_HEADS, PAST_LEN, A_HEAD_DIM), 1.0),
        'cache_attn_v': nrm(ks[6], (DEC_BATCH, N_ATTN_LAYERS, A_KV_HEADS, PAST_LEN, A_HEAD_DIM), 1.0),
        'c': nrm(ks[7], (DEC_BATCH, D_MODEL), 1.0),
        'c_ctx': nrm(ks[8], (D_MODEL,), 1.0),
        'ada_w': nrm(ks[9], (DEPTH, D_MODEL, 6 * D_MODEL), D_MODEL ** -0.5),
        'ada_b': nrm(ks[10], (DEPTH, 6 * D_MODEL), 0.01),
        'norm_g': 1.0 + nrm(ks[11], (DEPTH, 2, D_MODEL), 0.01),
        'final_g': 1.0 + nrm(ks[12], (D_MODEL,), 0.01),
        'mlstm_w_in': nrm(ks[13], (N_MLSTM_LAYERS, D_MODEL, M_IN_DIM), D_MODEL ** -0.5),
        'mlstm_conv_w': nrm(ks[14], (N_MLSTM_LAYERS, M_CONV_W, M_QK_DIM), M_CONV_W ** -0.5),
        'mlstm_conv_b': nrm(ks[15], (N_MLSTM_LAYERS, M_QK_DIM), 0.01),
        'mlstm_gate_b': gate_base + nrm(ks[16], (N_MLSTM_LAYERS, 4 * M_HEADS), 0.01),
        'mlstm_head_g': 1.0 + nrm(ks[17], (N_MLSTM_LAYERS, M_V_DIM), 0.01),
        'mlstm_w_out': nrm(ks[18], (N_MLSTM_LAYERS, M_V_DIM, D_MODEL), M_V_DIM ** -0.5),
        'attn_w_qkv': nrm(ks[19], (N_ATTN_LAYERS, D_MODEL, A_QKV_DIM), D_MODEL ** -0.5),
        'attn_q_g': 1.0 + nrm(ks[20], (N_ATTN_LAYERS, A_HEAD_DIM), 0.01),
        'attn_k_g': 1.0 + nrm(ks[21], (N_ATTN_LAYERS, A_HEAD_DIM), 0.01),
        'attn_w_out': nrm(ks[22], (N_ATTN_LAYERS, A_HEADS * A_HEAD_DIM, D_MODEL), (A_HEADS * A_HEAD_DIM) ** -0.5),
        'peer_w_q': nrm(ks[23], (DEPTH, D_MODEL, P_HEADS * P_DKEY), D_MODEL ** -0.5),
        'peer_subkeys': nrm(ks[24], (DEPTH, P_HEADS, 2, P_NKEYS, P_DKEY // 2), (P_DKEY // 2) ** -0.5),
        'peer_u': nrm(ks[25], (DEPTH, P_NEXPERTS, D_MODEL), D_MODEL ** -0.5),
        'peer_v': nrm(ks[26], (DEPTH, P_NEXPERTS, D_MODEL), P_HEADS ** -0.5),
    }


def reference(x_prompt, x_sample, state_mlstm_C, state_mlstm_n, state_mlstm_m, cache_attn_k, cache_attn_v,
              c, c_ctx, ada_w, ada_b, norm_g, final_g, mlstm_w_in, mlstm_conv_w, mlstm_conv_b, mlstm_gate_b,
              mlstm_head_g, mlstm_w_out, attn_w_qkv, attn_q_g, attn_k_g, attn_w_out,
              peer_w_q, peer_subkeys, peer_u, peer_v):
    f32 = lambda t: t.astype(jnp.float32)

    x = x_prompt
    B = x.shape[0]
    st_C, st_n, st_m, kv_k, kv_v = [], [], [], [], []
    for i in range(DEPTH):
        j = i // N_MIXERS
        sh1, sc1, g1, sh2, sc2, g2 = ada_mod(c_ctx, ada_w[i], ada_b[i])
        h = modulate(x, norm_g[i, 0], sh1, sc1)
        if i % N_MIXERS == 0:
            q, k, v, o, gf, gb = mlstm_project(h, mlstm_w_in[j], mlstm_conv_w[j], mlstm_conv_b[j], mlstm_gate_b[j])
            zero = (jnp.zeros((B, M_HEADS, M_DQK, M_DV), jnp.float32),
                    jnp.zeros((B, M_HEADS, M_DQK), jnp.float32),
                    jnp.zeros((B, M_HEADS), jnp.float32))
            hm, sf, sb = mlstm_bidirectional(q, k, v, gf, gb, zero, zero)
            out = mlstm_output(hm, o, mlstm_head_g[j], mlstm_w_out[j])
            st_C.append(jnp.stack([sf[0], sb[0]], axis=1))
            st_n.append(jnp.stack([sf[1], sb[1]], axis=1))
            st_m.append(jnp.stack([sf[2], sb[2]], axis=1))
        else:
            q, k, v = attn_project(h, attn_w_qkv[j], attn_q_g[j], attn_k_g[j])
            out = block_attention(q, k, v) @ attn_w_out[j]
            kv_k.append(k)
            kv_v.append(v)
        x = x + g1 * out
        x = x + g2 * peer_ffn(modulate(x, norm_g[i, 1], sh2, sc2), peer_w_q[i], peer_subkeys[i], peer_u[i], peer_v[i])
    y_prompt = rms_norm(x, final_g)
    new_mlstm_C = jnp.stack(st_C, axis=1).astype(x.dtype)
    new_mlstm_n = jnp.stack(st_n, axis=1).astype(x.dtype)
    new_mlstm_m = jnp.stack(st_m, axis=1).astype(x.dtype)
    new_attn_k = jnp.stack(kv_k, axis=1)
    new_attn_v = jnp.stack(kv_v, axis=1)

    x = x_sample
    n_rows = x.shape[1] // GRID_W
    cos, sin = axial_rope_tables(n_rows)
    for i in range(DEPTH):
        j = i // N_MIXERS
        sh1, sc1, g1, sh2, sc2, g2 = ada_mod(c, ada_w[i], ada_b[i])
        h = modulate(x, norm_g[i, 1 - 1], sh1, sc1)
        if i % N_MIXERS == 0:
            q, k, v, o, gf, gb = mlstm_project(h, mlstm_w_in[j], mlstm_conv_w[j], mlstm_conv_b[j], mlstm_gate_b[j])
            init_f = (f32(state_mlstm_C[:, j, 0]), f32(state_mlstm_n[:, j, 0]), f32(state_mlstm_m[:, j, 0]))
            init_b = (f32(state_mlstm_C[:, j, 1]), f32(state_mlstm_n[:, j, 1]), f32(state_mlstm_m[:, j, 1]))
            hm, _, _ = mlstm_bidirectional(q, k, v, gf, gb, init_f, init_b)
            out = mlstm_output(hm, o, mlstm_head_g[j], mlstm_w_out[j])
        else:
            q, k, v = attn_project(h, attn_w_qkv[j], attn_q_g[j], attn_k_g[j])
            q = apply_axial_rope(q, cos, sin)
            k = apply_axial_rope(k, cos, sin)
            k_all = jnp.concatenate([cache_attn_k[:, j].astype(k.dtype), k], axis=2)
            v_all = jnp.concatenate([cache_attn_v[:, j].astype(v.dtype), v], axis=2)
            out = block_attention(q, k_all, v_all) @ attn_w_out[j]
        x = x + g1 * out
        x = x + g2 * peer_ffn(modulate(x, norm_g[i, 1], sh2, sc2), peer_w_q[i], peer_subkeys[i], peer_u[i], peer_v[i])
    y_sample = rms_norm(x, final_g)

    return (y_prompt, y_sample, new_mlstm_C, new_mlstm_n, new_mlstm_m, new_attn_k, new_attn_v)
```

```python
import functools

import jax
import jax.numpy as jnp
from jax import lax
from jax.experimental import pallas as pl
from jax.experimental.pallas import tpu as pltpu

F32 = jnp.float32
BF16 = jnp.bfloat16

EPS = 1e-6
D_MODEL = 1024
GRID_W = 64
ROPE_THETA = 10000.0

M_HEADS = 4
M_DQK = 128
M_DV = 256
M_CHUNK = 128
M_QK_DIM = 2 * M_HEADS * M_DQK
M_V_DIM = M_HEADS * M_DV
M_MAIN_DIM = M_QK_DIM + 2 * M_V_DIM

A_HEADS = 16
A_KV_HEADS = 4
A_HEAD_DIM = 64
A_ROPE_AXIS = A_HEAD_DIM // 2

P_HEADS = 8
P_NKEYS = 128
P_DKEY = 256
P_TOPK = 16

LANES = 128
PROJ_TB = 256
PEER_TD = 512
PEER_EC = 1024
ATTN_QB = 512
VMEM_LIMIT = 56 * 1024 * 1024


def _cparams(*sem):
    return pltpu.CompilerParams(dimension_semantics=sem, vmem_limit_bytes=VMEM_LIMIT)


def _dg(a, b, ca=1, cb=0):
    return lax.dot_general(a, b, (((ca,), (cb,)), ((), ())), preferred_element_type=F32)


def _split2(x):
    hi = x.astype(BF16)
    lo = (x - hi.astype(F32)).astype(BF16)
    return hi, lo


def _mm3(a, b, ca=1, cb=0):
    ah, al = _split2(a)
    bh, bl = _split2(b)
    return _dg(ah, bh, ca, cb) + (_dg(ah, bl, ca, cb) + _dg(al, bh, ca, cb))


def _mm3w(a, wh, wl, ca=1, cb=0):
    ah, al = _split2(a)
    return _dg(ah, wh, ca, cb) + (_dg(ah, wl, ca, cb) + _dg(al, wh, ca, cb))


def _modulate(x, g, shift, scale):
    y = x * lax.rsqrt(jnp.mean(x * x, axis=-1, keepdims=True) + EPS)
    return (y * g) * (1.0 + scale) + shift


def _cond_map(base, per):
    if per is None:
        return lambda i, *_: (base, 0, 0)
    return lambda i, *_: (base + i // per, 0, 0)


def _ada_kernel(c_ref, w_ref, b_ref, o_ref):
    c = c_ref[...]
    a = c * jax.nn.sigmoid(c)
    o_ref[0] = _mm3(a, w_ref[0]) + b_ref[0]


def _ada_mods(cond, ada_w, ada_b):
    depth, d, n = ada_w.shape
    tn = 512
    rows = cond.shape[0]
    out = pl.pallas_call(
        _ada_kernel,
        grid=(depth, n // tn),
        in_specs=[
            pl.BlockSpec((rows, d), lambda l, j: (0, 0)),
            pl.BlockSpec((1, d, tn), lambda l, j: (l, 0, j)),
            pl.BlockSpec((1, 1, tn), lambda l, j: (l, 0, j)),
        ],
        out_specs=pl.BlockSpec((1, rows, tn), lambda l, j: (l, 0, j)),
        out_shape=jax.ShapeDtypeStruct((depth, rows, n), F32),
        compiler_params=_cparams("arbitrary", "arbitrary"),
        name="ada_mods",
    )(cond, ada_w, ada_b.reshape(depth, 1, n))
    return out.reshape(depth, rows, 6, d)


def _mproj_kernel(x_ref, m_ref, g_ref, w_ref, wgh_ref, wgl_ref, gb_ref, p_ref, gt_ref):
    h = _modulate(x_ref[...], g_ref[...], m_ref[0, 0:1, :], m_ref[0, 1:2, :])
    p_ref[...] = _dg(h.astype(BF16), w_ref[...])
    hh, hl = _split2(h)
    wh = wgh_ref[...]
    gt = _dg(wh, hh, 1, 1) + (_dg(wh, hl, 1, 1) + _dg(wgl_ref[...], hh, 1, 1))
    gt_ref[...] = gt + gb_ref[...]


def _mlstm_project(x, mods, cmap, g, w_main, wg_hi, wg_lo, gate_b):
    ntok = x.shape[0]
    tb = PROJ_TB
    ng = wg_hi.shape[0]
    return pl.pallas_call(
        _mproj_kernel,
        grid=(ntok // tb,),
        in_specs=[
            pl.BlockSpec((tb, D_MODEL), lambda i: (i, 0)),
            pl.BlockSpec((1, 6, D_MODEL), cmap),
            pl.BlockSpec((1, D_MODEL), lambda i: (0, 0)),
            pl.BlockSpec((D_MODEL, M_MAIN_DIM), lambda i: (0, 0)),
            pl.BlockSpec((ng, D_MODEL), lambda i: (0, 0)),
            pl.BlockSpec((ng, D_MODEL), lambda i: (0, 0)),
            pl.BlockSpec((ng, 1), lambda i: (0, 0)),
        ],
        out_specs=[
            pl.BlockSpec((tb, M_MAIN_DIM), lambda i: (i, 0)),
            pl.BlockSpec((ng, tb), lambda i: (0, i)),
        ],
        out_shape=[
            jax.ShapeDtypeStruct((ntok, M_MAIN_DIM), F32),
            jax.ShapeDtypeStruct((ng, ntok), F32),
        ],
        compiler_params=_cparams("arbitrary"),
        name="mlstm_project",
    )(x, mods, g, w_main, wg_hi, wg_lo, gate_b)


def _scan_kernel(seq, state_out, qp_ref, kp_ref, v_ref, g_ref, cwq_ref, cwk_ref, cbq_ref, cbk_ref,
                 c0_ref, n0_ref, m0_ref, hn_ref, *rest):
    if state_out:
        co_ref, no_ref, mo_ref, q_s, k_s, h_s, c_s = rest
    else:
        q_s, k_s, h_s, c_s = rest
    nc = seq // M_CHUNK
    rows = lax.broadcasted_iota(jnp.int32, (seq, 1), 0)

    def conv(x, w_ref, b_ref):
        xm1 = jnp.where(rows == 0, 0.0, pltpu.roll(x, 1, 0))
        xp1 = jnp.where(rows == seq - 1, 0.0, pltpu.roll(x, seq - 1, 0))
        y = xm1 * w_ref[0:1, :] + x * w_ref[1:2, :] + xp1 * w_ref[2:3, :] + b_ref[...]
        return y * jax.nn.sigmoid(y)

    q_s[...] = conv(qp_ref[...], cwq_ref, cbq_ref)
    k_s[...] = conv(kp_ref[...], cwk_ref, cbk_ref) * (M_DQK ** -0.5)

    ti = lax.broadcasted_iota(jnp.int32, (M_CHUNK, M_CHUNK), 0)
    si = lax.broadcasted_iota(jnp.int32, (M_CHUNK, M_CHUNK), 1)
    eye = ti == si

    def row2col(r):
        return jnp.sum(jnp.where(eye, r, 0.0), axis=1, keepdims=True)

    def col2row(c):
        return jnp.sum(jnp.where(eye, c, 0.0), axis=0, keepdims=True)

    def run_dir(dirn):
        mask = (si <= ti) if dirn == 0 else (si >= ti)
        c_s[...] = c0_ref[0, 0, dirn, 0]

        def body(it, carry):
            n, m = carry
            c = it if dirn == 0 else nc - 1 - it
            r0 = pl.multiple_of(c * M_CHUNK, M_CHUNK)
            g4 = g_ref[0, c]
            ig = g4[2 * dirn:2 * dirn + 1, :]
            fp = g4[2 * dirn + 1:2 * dirn + 2, :]
            lf = jnp.minimum(fp, 0.0) - jnp.log1p(jnp.exp(-jnp.abs(fp)))
            b_col = jnp.sum(jnp.where(mask, lf, 0.0), axis=1, keepdims=True)
            b_row = col2row(b_col)
            b_end = jnp.sum(lf, axis=1, keepdims=True)
            qc = q_s[pl.ds(r0, M_CHUNK), :]
            kc = k_s[pl.ds(r0, M_CHUNK), :]
            vc = v_ref[pl.ds(r0, M_CHUNK), :]
            d = jnp.where(mask, b_col - b_row + ig, -jnp.inf)
            inter = b_col + m
            m_t = jnp.maximum(inter, jnp.max(d, axis=1, keepdims=True))
            w = jnp.exp(d - m_t) * _mm3(qc, kc, 1, 1)
            a = jnp.exp(inter - m_t)
            c_old = c_s[...]
            num = a * _mm3(qc, c_old) + _mm3(w, vc)
            den = a * jnp.sum(qc * n, axis=1, keepdims=True) + jnp.sum(w, axis=1, keepdims=True)
            h = num / jnp.maximum(jnp.abs(den), jnp.exp(-m_t))
            if dirn == 0:
                h_s[pl.ds(r0, M_CHUNK), :] = h
            else:
                h_s[pl.ds(r0, M_CHUNK), :] += h
            to_end = b_end - b_row + ig
            m_new = jnp.maximum(b_end + m, jnp.max(to_end, axis=1, keepdims=True))
            decay = jnp.exp(b_end + m - m_new)
            kw = kc * row2col(jnp.exp(to_end - m_new))
            c_s[...] = decay * c_old + _mm3(kw.T, vc)
            n_new = decay * n + jnp.sum(kw, axis=0, keepdims=True)
            return n_new, m_new

        n_f, m_f = lax.fori_loop(0, nc, body, (n0_ref[0, 0, dirn, 0], m0_ref[0, 0, dirn, 0]))
        if state_out:
            co_ref[0, 0, dirn, 0] = c_s[...]
            no_ref[0, 0, dirn, 0] = n_f
            mo_ref[0, 0, dirn, 0] = m_f

    run_dir(0)
    run_dir(1)
    hm = h_s[...]
    hn_ref[...] = hm * lax.rsqrt(jnp.mean(hm * hm, axis=-1, keepdims=True) + EPS)


def _mlstm_scan(p_main, gates, conv_w, conv_b, c0, n0, m0, nseq, seq, state_out):
    ntok = nseq * seq
    nc = seq // M_CHUNK
    qblk = M_DQK
    st5 = lambda b, h: (b, 0, 0, h, 0, 0)
    in_specs = [
        pl.BlockSpec((seq, qblk), lambda b, h: (b, h)),
        pl.BlockSpec((seq, qblk), lambda b, h: (b, M_HEADS + h)),
        pl.BlockSpec((seq, M_DV), lambda b, h: (b, M_QK_DIM // M_DV + h)),
        pl.BlockSpec((1, nc, 4, M_CHUNK), lambda b, h: (h, b, 0, 0)),
        pl.BlockSpec((3, qblk), lambda b, h: (0, h)),
        pl.BlockSpec((3, qblk), lambda b, h: (0, M_HEADS + h)),
        pl.BlockSpec((1, qblk), lambda b, h: (0, h)),
        pl.BlockSpec((1, qblk), lambda b, h: (0, M_HEADS + h)),
        pl.BlockSpec((1, 1, 2, 1, M_DQK, M_DV), st5),
        pl.BlockSpec((1, 1, 2, 1, 1, M_DQK), st5),
        pl.BlockSpec((1, 1, 2, 1, 1, 1), st5),
    ]
    out_specs = [pl.BlockSpec((seq, M_DV), lambda b, h: (b, h))]
    out_shape = [jax.ShapeDtypeStruct((ntok, M_V_DIM), F32)]
    if state_out:
        out_specs += [
            pl.BlockSpec((1, 1, 2, 1, M_DQK, M_DV), st5),
            pl.BlockSpec((1, 1, 2, 1, 1, M_DQK), st5),
            pl.BlockSpec((1, 1, 2, 1, 1, 1), st5),
        ]
        out_shape += [
            jax.ShapeDtypeStruct((nseq, 1, 2, M_HEADS, M_DQK, M_DV), F32),
            jax.ShapeDtypeStruct((nseq, 1, 2, M_HEADS, 1, M_DQK), F32),
            jax.ShapeDtypeStruct((nseq, 1, 2, M_HEADS, 1, 1), F32),
        ]
    return pl.pallas_call(
        functools.partial(_scan_kernel, seq, state_out),
        grid=(nseq, M_HEADS),
        in_specs=in_specs,
        out_specs=out_specs,
        out_shape=out_shape,
        scratch_shapes=[
            pltpu.VMEM((seq, M_DQK), F32),
            pltpu.VMEM((seq, M_DQK), F32),
            pltpu.VMEM((seq, M_DV), F32),
            pltpu.VMEM((M_DQK, M_DV), F32),
        ],
        compiler_params=_cparams("arbitrary", "arbitrary"),
        name="mlstm_scan_ctx" if state_out else "mlstm_scan_smp",
    )(p_main, p_main, p_main, gates, conv_w, conv_w, conv_b, conv_b, c0, n0, m0)


def _mout_kernel(x_ref, m_ref, hn_ref, o_ref, hg_ref, w_ref, y_ref):
    t = hn_ref[...] * hg_ref[...] * jax.nn.sigmoid(o_ref[...])
    y_ref[...] = x_ref[...] + m_ref[0, 2:3, :] * _dg(t.astype(BF16), w_ref[...])


def _mlstm_out(x, mods, cmap, hn, p_main, head_g, w_out):
    ntok = x.shape[0]
    tb = PROJ_TB
    return pl.pallas_call(
        _mout_kernel,
        grid=(ntok // tb,),
        in_specs=[
            pl.BlockSpec((tb, D_MODEL), lambda i: (i, 0)),
            pl.BlockSpec((1, 6, D_MODEL), cmap),
            pl.BlockSpec((tb, M_V_DIM), lambda i: (i, 0)),
            pl.BlockSpec((tb, M_V_DIM), lambda i: (i, (M_QK_DIM + M_V_DIM) // M_V_DIM)),
            pl.BlockSpec((1, M_V_DIM), lambda i: (0, 0)),
            pl.BlockSpec((M_V_DIM, D_MODEL), lambda i: (0, 0)),
        ],
        out_specs=pl.BlockSpec((tb, D_MODEL), lambda i: (i, 0)),
        out_shape=jax.ShapeDtypeStruct((ntok, D_MODEL), F32),
        compiler_params=_cparams("arbitrary"),
        name="mlstm_out",
    )(x, mods, hn, p_main, head_g, w_out)


def _aout_kernel(x_ref, m_ref, a_ref, w_ref, y_ref):
    y_ref[...] = x_ref[...] + m_ref[0, 2:3, :] * _dg(a_ref[...].astype(BF16), w_ref[...])


def _attn_out(x, mods, cmap, att, w_out):
    ntok = x.shape[0]
    tb = PROJ_TB
    return pl.pallas_call(
        _aout_kernel,
        grid=(ntok // tb,),
        in_specs=[
            pl.BlockSpec((tb, D_MODEL), lambda i: (i, 0)),
            pl.BlockSpec((1, 6, D_MODEL), cmap),
            pl.BlockSpec((tb, D_MODEL), lambda i: (i, 0)),
            pl.BlockSpec((D_MODEL, D_MODEL), lambda i: (0, 0)),
        ],
        out_specs=pl.BlockSpec((tb, D_MODEL), lambda i: (i, 0)),
        out_shape=jax.ShapeDtypeStruct((ntok, D_MODEL), F32),
        compiler_params=_cparams("arbitrary"),
        name="attn_out",
    )(x, mods, att, w_out)


def _aproj_kernel(rope, cache_out, x_ref, m_ref, g_ref, w_ref, gs_ref, qg_ref, kg_ref, *rest):
    rest = list(rest)
    if rope:
        cos_ref, sin_ref = rest[:2]
        rest = rest[2:]
    q_ref, k_ref, v_ref = rest[:3]
    rest = rest[3:]
    h = _modulate(x_ref[...], g_ref[...], m_ref[0, 0:1, :], m_ref[0, 1:2, :])
    p = _dg(h.astype(BF16), w_ref[...])
    nq = A_HEADS * A_HEAD_DIM
    nk = 2 * A_KV_HEADS * A_HEAD_DIM
    gs = gs_ref[...]
    lane = lax.broadcasted_iota(jnp.int32, (1, LANES), 1)
    first_half = (lane % A_ROPE_AXIS) < (A_ROPE_AXIS // 2)

    def norm_rope(xs, gain):
        sq = xs * xs
        s1 = sq.astype(BF16)
        r1 = sq - s1.astype(F32)
        s2 = r1.astype(BF16)
        s3 = (r1 - s2.astype(F32)).astype(BF16)
        ss = _dg(s1, gs) + (_dg(s2, gs) + _dg(s3, gs))
        y = xs * lax.rsqrt(ss * (1.0 / A_HEAD_DIM) + EPS) * gain
        if rope:
            half = A_ROPE_AXIS // 2
            swapped = jnp.where(first_half, pltpu.roll(y, LANES - half, 1), pltpu.roll(y, half, 1))
            y = y * cos_ref[...] + swapped * sin_ref[...]
        return y

    for s in range(nq // LANES):
        q_ref[:, s * LANES:(s + 1) * LANES] = norm_rope(p[:, s * LANES:(s + 1) * LANES], qg_ref[...])
    for s in range(nk // LANES):
        kn = norm_rope(p[:, nq + s * LANES:nq + (s + 1) * LANES], kg_ref[...])
        k_ref[:, s * LANES:(s + 1) * LANES] = kn
        vs = p[:, nq + nk + s * LANES:nq + nk + (s + 1) * LANES]
        v_ref[:, s * LANES:(s + 1) * LANES] = vs
        if cache_out:
            kc_ref, vc_ref = rest
            kc_ref[0, 0, s] = kn[:, :A_HEAD_DIM]
            vc_ref[0, 0, s] = vs[:, :A_HEAD_DIM]


def _attn_project(x, mods, cmap, g, w_qkv2, gsum, qg, kg, rope_tabs, nseq, seq, cache_out):
    ntok = x.shape[0]
    tb = PROJ_TB
    nq = A_HEADS * A_HEAD_DIM
    nk = 2 * A_KV_HEADS * A_HEAD_DIM
    per = seq // tb
    rope = rope_tabs is not None
    in_specs = [
        pl.BlockSpec((tb, D_MODEL), lambda i: (i, 0)),
        pl.BlockSpec((1, 6, D_MODEL), cmap),
        pl.BlockSpec((1, D_MODEL), lambda i: (0, 0)),
        pl.BlockSpec((D_MODEL, nq + 2 * nk), lambda i: (0, 0)),
        pl.BlockSpec((LANES, LANES), lambda i: (0, 0)),
        pl.BlockSpec((1, LANES), lambda i: (0, 0)),
        pl.BlockSpec((1, LANES), lambda i: (0, 0)),
    ]
    args = [x, mods, g, w_qkv2, gsum, qg, kg]
    if rope:
        in_specs += [pl.BlockSpec((tb, LANES), lambda i: (i % per, 0))] * 2
        args += list(rope_tabs)
    out_specs = [
        pl.BlockSpec((tb, nq), lambda i: (i, 0)),
        pl.BlockSpec((tb, nk), lambda i: (i, 0)),
        pl.BlockSpec((tb, nk), lambda i: (i, 0)),
    ]
    out_shape = [
        jax.ShapeDtypeStruct((ntok, nq), F32),
        jax.ShapeDtypeStruct((ntok, nk), F32),
        jax.ShapeDtypeStruct((ntok, nk), F32),
    ]
    if cache_out:
        assert seq == tb
        cspec = pl.BlockSpec((1, 1, A_KV_HEADS, seq, A_HEAD_DIM), lambda i: (i, 0, 0, 0, 0))
        cshape = jax.ShapeDtypeStruct((nseq, 1, A_KV_HEADS, seq, A_HEAD_DIM), F32)
        out_specs += [cspec, cspec]
        out_shape += [cshape, cshape]
    return pl.pallas_call(
        functools.partial(_aproj_kernel, rope, cache_out),
        grid=(ntok // tb,),
        in_specs=in_specs,
        out_specs=out_specs,
        out_shape=out_shape,
        compiler_params=_cparams("arbitrary"),
        name="attn_project_ctx" if cache_out else "attn_project_smp",
    )(*args)


def _attn_kernel(cached, q_ref, k_ref, v_ref, *rest):
    if cached:
        kc_ref, vc_ref, o_ref = rest
    else:
        (o_ref,) = rest
    lane = lax.broadcasted_iota(jnp.int32, (1, LANES), 1)
    q = q_ref[...] * (A_HEAD_DIM ** -0.5)
    k = k_ref[...].astype(BF16)
    v = v_ref[...]
    if cached:
        kc = kc_ref[0, 0].astype(BF16)
        vc = vc_ref[0, 0]
    acc = jnp.zeros(q.shape, F32)
    for e in range(2):
        sel = (lane < A_HEAD_DIM) if e == 0 else (lane >= A_HEAD_DIM)
        qe = jnp.where(sel, q, 0.0).astype(BF16)
        s = _dg(qe, k, 1, 1)
        mx = jnp.max(s, axis=1, keepdims=True)
        if cached:
            sc = _dg(qe, kc, 1, 1)
            mx = jnp.maximum(mx, jnp.max(sc, axis=1, keepdims=True))
        p = jnp.exp(s - mx)
        l = jnp.sum(p, axis=1, keepdims=True)
        o = _dg(p.astype(BF16), jnp.where(sel, v, 0.0).astype(BF16))
        if cached:
            pc = jnp.exp(sc - mx)
            l = l + jnp.sum(pc, axis=1, keepdims=True)
            o = o + _dg(pc.astype(BF16), jnp.where(sel, vc, 0.0).astype(BF16))
        acc = acc + o / l
    o_ref[...] = acc


def _attention(q, k2, v2, cache, nseq, seq):
    ntok = q.shape[0]
    qb = min(ATTN_QB, seq)
    nqb = seq // qb
    npair = A_HEADS // 2
    in_specs = [
        pl.BlockSpec((qb, LANES), lambda b, hp, j: (b * nqb + j, hp)),
        pl.BlockSpec((seq, LANES), lambda b, hp, j: (b, hp // 2)),
        pl.BlockSpec((seq, LANES), lambda b, hp, j: (b, hp // 2)),
    ]
    args = [q, k2, v2]
    if cache is not None:
        past = cache[0].shape[2]
        cspec = pl.BlockSpec((1, 1, past, LANES), lambda b, hp, j: (b, hp // 2, 0, 0))
        in_specs += [cspec, cspec]
        args += list(cache)
    return pl.pallas_call(
        functools.partial(_attn_kernel, cache is not None),
        grid=(nseq, npair, nqb),
        in_specs=in_specs,
        out_specs=pl.BlockSpec((qb, LANES), lambda b, hp, j: (b * nqb + j, hp)),
        out_shape=jax.ShapeDtypeStruct((ntok, A_HEADS * A_HEAD_DIM), F32),
        compiler_params=_cparams("arbitrary", "arbitrary", "arbitrary"),
        name="attention_smp" if cache is not None else "attention_ctx",
    )(*args)


def _top_rows(s, k):
    nrow = s.shape[0]
    iota = lax.broadcasted_iota(jnp.int32, s.shape, 0)
    cur = s
    outs = []
    for _ in range(k):
        m = jnp.max(cur, axis=0, keepdims=True)
        outs.append(m)
        first = jnp.min(jnp.where(cur == m, iota, nrow), axis=0, keepdims=True)
        cur = jnp.where(iota == first, -jnp.inf, cur)
    return outs


def _cand_pairs():
    k1 = P_TOPK + 1
    return [(i, j) for i in range(k1) for j in range(k1) if (i + 1) * (j + 1) <= k1]


def _pproj_kernel(x_ref, m_ref, g_ref, wqh_ref, wql_ref, skh_ref, skl_ref,
                  ht_ref, thr_ref, e1_ref, s2_ref, e2_ref, cand_ref):
    h = _modulate(x_ref[...], g_ref[...], m_ref[0, 3:4, :], m_ref[0, 4:5, :])
    ht_ref[...] = h.T.astype(BF16)
    q = _mm3w(h, wqh_ref[...], wql_ref[...])
    pairs = _cand_pairs()
    k1 = P_TOPK + 1
    cand_ref[...] = jnp.full(cand_ref.shape, -jnp.inf, F32)
    for p in range(P_HEADS):
        sc = []
        for hf in range(2):
            ph = 2 * p + hf
            qh, ql = _split2(q[:, ph * LANES:(ph + 1) * LANES])
            kh = skh_ref[ph]
            sc.append(_dg(kh, qh, 1, 1) + (_dg(kh, ql, 1, 1) + _dg(skl_ref[ph], qh, 1, 1)))
        s1, s2 = sc
        a = _top_rows(s1, k1)
        b = _top_rows(s2, k1)
        for r, (i, j) in enumerate(pairs):
            cand_ref[r:r + 1, :] = a[i] + b[j]
        v = _top_rows(cand_ref[...], k1)
        tau = 0.5 * (v[P_TOPK - 1] + v[P_TOPK])
        z = jnp.ones_like(tau)
        for kk in range(1, P_TOPK):
            z = z + jnp.exp(v[kk] - v[0])
        thr_ref[p] = tau - s1
        e1_ref[p] = jnp.exp(s1 - a[0]) / z
        s2_ref[p] = s2
        e2_ref[p] = jnp.exp(s2 - b[0])


def _peer_project(x, mods, cmap, g, wq_hi, wq_lo, sk_hi, sk_lo):
    ntok = x.shape[0]
    tb = PROJ_TB
    nq = P_HEADS * P_DKEY
    ncand = -(-len(_cand_pairs()) // 8) * 8
    sel_spec = pl.BlockSpec((P_HEADS, P_NKEYS, tb), lambda i: (0, 0, i))
    sel_shape = jax.ShapeDtypeStruct((P_HEADS, P_NKEYS, ntok), F32)
    return pl.pallas_call(
        _pproj_kernel,
        grid=(ntok // tb,),
        in_specs=[
            pl.BlockSpec((tb, D_MODEL), lambda i: (i, 0)),
            pl.BlockSpec((1, 6, D_MODEL), cmap),
            pl.BlockSpec((1, D_MODEL), lambda i: (0, 0)),
            pl.BlockSpec((D_MODEL, nq), lambda i: (0, 0)),
            pl.BlockSpec((D_MODEL, nq), lambda i: (0, 0)),
            pl.BlockSpec((2 * P_HEADS, P_NKEYS, P_DKEY // 2), lambda i: (0, 0, 0)),
            pl.BlockSpec((2 * P_HEADS, P_NKEYS, P_DKEY // 2), lambda i: (0, 0, 0)),
        ],
        out_specs=[pl.BlockSpec((D_MODEL, tb), lambda i: (0, i)), sel_spec, sel_spec, sel_spec, sel_spec],
        out_shape=[jax.ShapeDtypeStruct((D_MODEL, ntok), BF16), sel_shape, sel_shape, sel_shape, sel_shape],
        scratch_shapes=[pltpu.VMEM((ncand, tb), F32)],
        compiler_params=_cparams("arbitrary"),
        name="peer_project",
    )(x, mods, g, wq_hi, wq_lo, sk_hi, sk_lo)


def _pdense_kernel(final, x_ref, m_ref, ht_ref, thr_ref, e1_ref, s2_ref, e2_ref, u_ref, vt_ref, fg_ref,
                   y_ref, act_s, gt_s, acc_s):
    j = pl.program_id(1)
    ec, td = act_s.shape
    bsub = 64

    @pl.when(j == 0)
    def _():
        acc_s[...] = jnp.zeros(acc_s.shape, F32)

    act_s[...] = _dg(u_ref[...], ht_ref[...])

    def lane_body(l, carry):
        ls = pl.ds(pl.multiple_of(l * LANES, LANES), LANES)
        for al in range(ec // P_NKEYS):
            for bh in range(P_NKEYS // bsub):
                bs = slice(bh * bsub, (bh + 1) * bsub)
                w = jnp.zeros((bsub, LANES), F32)
                for p in range(P_HEADS):
                    thr = thr_ref[p, al:al + 1, ls]
                    e1 = e1_ref[p, al:al + 1, ls]
                    w = w + jnp.where(s2_ref[p, bs, ls] >= thr, e2_ref[p, bs, ls], 0.0) * e1
                rs = slice(al * P_NKEYS + bh * bsub, al * P_NKEYS + (bh + 1) * bsub)
                xa = act_s[rs, ls]
                gel = 0.5 * xa * (1.0 + lax.erf(xa * (2.0 ** -0.5)))
                gt_s[rs, ls] = (w * gel).astype(BF16)
        return carry

    lax.fori_loop(0, td // LANES, lane_body, 0)
    acc_s[...] += _dg(vt_ref[...], gt_s[...])

    @pl.when(j == pl.num_programs(1) - 1)
    def _():
        xn = x_ref[...] + m_ref[0, 5:6, :] * acc_s[...].T
        if final:
            xn = xn * lax.rsqrt(jnp.mean(xn * xn, axis=-1, keepdims=True) + EPS) * fg_ref[...]
        y_ref[...] = xn


def _peer_dense(x, mods, cmap, ht, thr, e1, s2, e2, u, vt, final_g, final):
    ntok = x.shape[0]
    td = PEER_TD
    ec = PEER_EC
    nexp = u.shape[0]
    sel_spec = pl.BlockSpec((P_HEADS, P_NKEYS, td), lambda i, j: (0, 0, i))
    row_spec = pl.BlockSpec((P_HEADS, ec // P_NKEYS, td), lambda i, j: (0, j, i))
    return pl.pallas_call(
        functools.partial(_pdense_kernel, final),
        grid=(ntok // td, nexp // ec),
        in_specs=[
            pl.BlockSpec((td, D_MODEL), lambda i, j: (i, 0)),
            pl.BlockSpec((1, 6, D_MODEL), cmap),
            pl.BlockSpec((D_MODEL, td), lambda i, j: (0, i)),
            row_spec, row_spec, sel_spec, sel_spec,
            pl.BlockSpec((ec, D_MODEL), lambda i, j: (j, 0)),
            pl.BlockSpec((D_MODEL, ec), lambda i, j: (0, j)),
            pl.BlockSpec((1, D_MODEL), lambda i, j: (0, 0)),
        ],
        out_specs=pl.BlockSpec((td, D_MODEL), lambda i, j: (i, 0)),
        out_shape=jax.ShapeDtypeStruct((ntok, D_MODEL), F32),
        scratch_shapes=[
            pltpu.VMEM((ec, td), F32),
            pltpu.VMEM((ec, td), BF16),
            pltpu.VMEM((D_MODEL, td), F32),
        ],
        compiler_params=_cparams("arbitrary", "arbitrary"),
        name="peer_dense",
    )(x, mods, ht, thr, e1, s2, e2, u, vt, final_g)


def _rope_tables(seq):
    n_rows = seq // GRID_W
    rows = jnp.repeat(jnp.arange(n_rows, dtype=F32), GRID_W)
    cols = jnp.tile(jnp.arange(GRID_W, dtype=F32), n_rows)
    half = A_ROPE_AXIS // 2
    inv_freq = ROPE_THETA ** (-jnp.arange(half, dtype=F32) / half)
    ar = rows[:, None] * inv_freq
    ac = cols[:, None] * inv_freq
    cos = jnp.concatenate([jnp.cos(ar), jnp.cos(ar), jnp.cos(ac), jnp.cos(ac)], axis=-1)
    sin = jnp.concatenate([-jnp.sin(ar), jnp.sin(ar), -jnp.sin(ac), jnp.sin(ac)], axis=-1)
    return jnp.tile(cos, (1, LANES // A_HEAD_DIM)), jnp.tile(sin, (1, LANES // A_HEAD_DIM))


def _hi_lo(w):
    hi = w.astype(BF16)
    return hi, (w - hi.astype(F32)).astype(BF16)


def kernel(x_prompt, x_sample, state_mlstm_C, state_mlstm_n, state_mlstm_m, cache_attn_k, cache_attn_v, c, c_ctx, ada_w, ada_b, norm_g, final_g, mlstm_w_in, mlstm_conv_w, mlstm_conv_b, mlstm_gate_b, mlstm_head_g, mlstm_w_out, attn_w_qkv, attn_q_g, attn_k_g, attn_w_out, peer_w_q, peer_subkeys, peer_u, peer_v):
    nb, seq_c, d = x_prompt.shape
    ndb, seq_s, _ = x_sample.shape
    assert d == D_MODEL and seq_c % PROJ_TB == 0 and seq_s % PROJ_TB == 0
    assert (nb * seq_c) % PEER_TD == 0 and seq_s % PEER_TD == 0

    nrow = -(-(1 + ndb) // 8) * 8
    cond = jnp.concatenate([c_ctx[None, :], c, jnp.zeros((nrow - 1 - ndb, d), F32)], axis=0)
    mods_all = _ada_mods(cond, ada_w, ada_b)

    groups = [
        dict(x=x_prompt.reshape(nb * seq_c, d), nseq=nb, seq=seq_c, ctx=True),
        dict(x=x_sample.reshape(ndb * seq_s, d), nseq=ndb, seq=seq_s, ctx=False),
    ]

    def cmap(gr, tb):
        return _cond_map(0, None) if gr["ctx"] else _cond_map(1, gr["seq"] // tb)

    outs = {}
    fg = final_g.reshape(1, d)

    w_in = mlstm_w_in[0]
    w_main = w_in[:, :M_MAIN_DIM].astype(BF16)
    perm = jnp.arange(4 * M_HEADS).reshape(4, M_HEADS).T.reshape(-1)
    wg_t = w_in[:, M_MAIN_DIM:].T[perm]
    wg_hi, wg_lo = _hi_lo(wg_t)
    gate_b = mlstm_gate_b[0][perm].reshape(-1, 1)
    w_mout = mlstm_w_out[0].astype(BF16)
    for gr in groups:
        nseq, seq = gr["nseq"], gr["seq"]
        mods = mods_all[0]
        p_main, gt = _mlstm_project(gr["x"], mods, cmap(gr, PROJ_TB), norm_g[0, 0].reshape(1, d),
                                    w_main, wg_hi, wg_lo, gate_b)
        nc = seq // M_CHUNK
        gates = gt.reshape(M_HEADS, 4, nseq * nc, M_CHUNK).transpose(0, 2, 1, 3)
        if gr["ctx"]:
            c0 = jnp.zeros((nseq, 1, 2, M_HEADS, M_DQK, M_DV), F32)
            n0 = jnp.zeros((nseq, 1, 2, M_HEADS, 1, M_DQK), F32)
            m0 = jnp.zeros((nseq, 1, 2, M_HEADS, 1, 1), F32)
        else:
            c0 = state_mlstm_C.astype(F32)
            n0 = state_mlstm_n.astype(F32).reshape(nseq, -1, 2, M_HEADS, 1, M_DQK)
            m0 = state_mlstm_m.astype(F32).reshape(nseq, -1, 2, M_HEADS, 1, 1)
        res = _mlstm_scan(p_main, gates, mlstm_conv_w[0], mlstm_conv_b[0].reshape(1, -1),
                          c0, n0, m0, nseq, seq, gr["ctx"])
        if gr["ctx"]:
            hn, c_new, n_new, m_new = res
            outs["C"] = c_new
            outs["n"] = n_new.reshape(nseq, 1, 2, M_HEADS, M_DQK)
            outs["m"] = m_new.reshape(nseq, 1, 2, M_HEADS)
        else:
            (hn,) = res
        gr["x"] = _mlstm_out(gr["x"], mods, cmap(gr, PROJ_TB), hn, p_main,
                             mlstm_head_g[0].reshape(1, -1), w_mout)

    def peer(layer, final):
        wq_hi, wq_lo = _hi_lo(peer_w_q[layer])
        sk_hi, sk_lo = _hi_lo(peer_subkeys[layer].reshape(2 * P_HEADS, P_NKEYS, P_DKEY // 2))
        u = peer_u[layer].astype(BF16)
        vt = peer_v[layer].astype(BF16).T
        for gr in groups:
            mods = mods_all[layer]
            ht, thr, e1, s2, e2 = _peer_project(gr["x"], mods, cmap(gr, PROJ_TB), norm_g[layer, 1].reshape(1, d),
                                                wq_hi, wq_lo, sk_hi, sk_lo)
            gr["x"] = _peer_dense(gr["x"], mods, cmap(gr, PEER_TD), ht, thr, e1, s2, e2, u, vt, fg, final)

    peer(0, False)

    wqkv = attn_w_qkv[0]
    nq = A_HEADS * A_HEAD_DIM
    nkv = A_KV_HEADS * A_HEAD_DIM
    dup = lambda w: jnp.tile(w.reshape(d, A_KV_HEADS, 1, A_HEAD_DIM), (1, 1, 2, 1)).reshape(d, 2 * nkv)
    w_qkv2 = jnp.concatenate([wqkv[:, :nq], dup(wqkv[:, nq:nq + nkv]), dup(wqkv[:, nq + nkv:])], axis=1).astype(BF16)
    li = jnp.arange(LANES)
    gsum = (li[:, None] // A_HEAD_DIM == li[None, :] // A_HEAD_DIM).astype(BF16)
    qg = jnp.tile(attn_q_g[0], LANES // A_HEAD_DIM).reshape(1, LANES)
    kg = jnp.tile(attn_k_g[0], LANES // A_HEAD_DIM).reshape(1, LANES)
    w_aout = attn_w_out[0].astype(BF16)
    for gr in groups:
        nseq, seq = gr["nseq"], gr["seq"]
        mods = mods_all[1]
        tabs = None if gr["ctx"] else _rope_tables(seq)
        res = _attn_project(gr["x"], mods, cmap(gr, PROJ_TB), norm_g[1, 0].reshape(1, d), w_qkv2, gsum, qg, kg,
                            tabs, nseq, seq, gr["ctx"])
        if gr["ctx"]:
            q, k2, v2, kc_new, vc_new = res
            outs["k"], outs["v"] = kc_new, vc_new
            cache = None
        else:
            q, k2, v2 = res
            cache = (jnp.tile(cache_attn_k[:, 0].astype(F32), (1, 1, 1, 2)),
                     jnp.tile(cache_attn_v[:, 0].astype(F32), (1, 1, 1, 2)))
        att = _attention(q, k2, v2, cache, nseq, seq)
        gr["x"] = _attn_out(gr["x"], mods, cmap(gr, PROJ_TB), att, w_aout)

    peer(1, True)

    y_prompt = groups[0]["x"].reshape(nb, seq_c, d)
    y_sample = groups[1]["x"].reshape(ndb, seq_s, d)
    return (y_prompt, y_sample, outs["C"], outs["n"], outs["m"], outs["k"], outs["v"])
```

```python
import functools

import jax
import jax.numpy as jnp
from jax import lax
from jax.experimental import pallas as pl
from jax.experimental.pallas import tpu as pltpu

F32 = jnp.float32
BF16 = jnp.bfloat16

EPS = 1e-6
LOG2E = 1.4426950408889634
D_MODEL = 1024
GRID_W = 64
ROPE_THETA = 10000.0

M_HEADS = 4
M_DQK = 128
M_DV = 256
M_CHUNK = 128
M_QK_DIM = 2 * M_HEADS * M_DQK
M_V_DIM = M_HEADS * M_DV
M_MAIN_DIM = M_QK_DIM + 2 * M_V_DIM

A_HEADS = 16
A_KV_HEADS = 4
A_HEAD_DIM = 64
A_ROPE_AXIS = A_HEAD_DIM // 2

P_HEADS = 8
P_NKEYS = 128
P_DKEY = 256
P_TOPK = 16

LANES = 128
PROJ_TB = 256
PEER_TD = 512
PEER_EC = 1024
ATTN_QB = 512
VMEM_LIMIT = 56 * 1024 * 1024


def _cparams(*sem):
    return pltpu.CompilerParams(dimension_semantics=sem, vmem_limit_bytes=VMEM_LIMIT)


def _dg(a, b, ca=1, cb=0):
    return lax.dot_general(a, b, (((ca,), (cb,)), ((), ())), preferred_element_type=F32)


def _split2(x):
    hi = x.astype(BF16)
    lo = (x - hi.astype(F32)).astype(BF16)
    return hi, lo


def _mm3(a, b, ca=1, cb=0):
    ah, al = _split2(a)
    bh, bl = _split2(b)
    return _dg(ah, bh, ca, cb) + (_dg(ah, bl, ca, cb) + _dg(al, bh, ca, cb))


def _mm3w(a, wh, wl, ca=1, cb=0):
    ah, al = _split2(a)
    return _dg(ah, wh, ca, cb) + (_dg(ah, wl, ca, cb) + _dg(al, wh, ca, cb))


def _modulate(x, g, shift, scale):
    y = x * lax.rsqrt(jnp.mean(x * x, axis=-1, keepdims=True) + EPS)
    return (y * g) * (1.0 + scale) + shift


def _cond_map(base, per):
    if per is None:
        return lambda i, *_: (base, 0, 0)
    return lambda i, *_: (base + i // per, 0, 0)


def _ada_kernel(c_ref, w_ref, b_ref, o_ref):
    c = c_ref[...]
    a = c * jax.nn.sigmoid(c)
    o_ref[0] = _mm3(a, w_ref[0]) + b_ref[0]


def _ada_mods(cond, ada_w, ada_b):
    depth, d, n = ada_w.shape
    tn = 512
    rows = cond.shape[0]
    out = pl.pallas_call(
        _ada_kernel,
        grid=(depth, n // tn),
        in_specs=[
            pl.BlockSpec((rows, d), lambda l, j: (0, 0)),
            pl.BlockSpec((1, d, tn), lambda l, j: (l, 0, j)),
            pl.BlockSpec((1, 1, tn), lambda l, j: (l, 0, j)),
        ],
        out_specs=pl.BlockSpec((1, rows, tn), lambda l, j: (l, 0, j)),
        out_shape=jax.ShapeDtypeStruct((depth, rows, n), F32),
        compiler_params=_cparams("arbitrary", "arbitrary"),
        name="ada_mods",
    )(cond, ada_w, ada_b.reshape(depth, 1, n))
    return out.reshape(depth, rows, 6, d)


def _mproj_kernel(x_ref, m_ref, g_ref, w_ref, wgh_ref, wgl_ref, gb_ref, p_ref, gt_ref):
    h = _modulate(x_ref[...], g_ref[...], m_ref[0, 0:1, :], m_ref[0, 1:2, :])
    p_ref[...] = _dg(h.astype(BF16), w_ref[...])
    hh, hl = _split2(h)
    wh = wgh_ref[...]
    gt = _dg(wh, hh, 1, 1) + (_dg(wh, hl, 1, 1) + _dg(wgl_ref[...], hh, 1, 1))
    gt_ref[...] = gt + gb_ref[...]


def _mlstm_project(x, mods, cmap, g, w_main, wg_hi, wg_lo, gate_b):
    ntok = x.shape[0]
    tb = PROJ_TB
    ng = wg_hi.shape[0]
    return pl.pallas_call(
        _mproj_kernel,
        grid=(ntok // tb,),
        in_specs=[
            pl.BlockSpec((tb, D_MODEL), lambda i: (i, 0)),
            pl.BlockSpec((1, 6, D_MODEL), cmap),
            pl.BlockSpec((1, D_MODEL), lambda i: (0, 0)),
            pl.BlockSpec((D_MODEL, M_MAIN_DIM), lambda i: (0, 0)),
            pl.BlockSpec((ng, D_MODEL), lambda i: (0, 0)),
            pl.BlockSpec((ng, D_MODEL), lambda i: (0, 0)),
            pl.BlockSpec((ng, 1), lambda i: (0, 0)),
        ],
        out_specs=[
            pl.BlockSpec((tb, M_MAIN_DIM), lambda i: (i, 0)),
            pl.BlockSpec((ng, tb), lambda i: (0, i)),
        ],
        out_shape=[
            jax.ShapeDtypeStruct((ntok, M_MAIN_DIM), F32),
            jax.ShapeDtypeStruct((ng, ntok), F32),
        ],
        compiler_params=_cparams("arbitrary"),
        name="mlstm_project",
    )(x, mods, g, w_main, wg_hi, wg_lo, gate_b)


def _scan_kernel(seq, state_out, qp_ref, kp_ref, v_ref, g_ref, cwq_ref, cwk_ref, cbq_ref, cbk_ref,
                 c0_ref, n0_ref, m0_ref, hn_ref, *rest):
    if state_out:
        co_ref, no_ref, mo_ref, q_s, k_s, h_s, c_s = rest
    else:
        q_s, k_s, h_s, c_s = rest
    nc = seq // M_CHUNK
    rows = lax.broadcasted_iota(jnp.int32, (seq, 1), 0)

    def conv(x, w_ref, b_ref):
        xm1 = jnp.where(rows == 0, 0.0, pltpu.roll(x, 1, 0))
        xp1 = jnp.where(rows == seq - 1, 0.0, pltpu.roll(x, seq - 1, 0))
        y = xm1 * w_ref[0:1, :] + x * w_ref[1:2, :] + xp1 * w_ref[2:3, :] + b_ref[...]
        return y * jax.nn.sigmoid(y)

    q_s[...] = conv(qp_ref[...], cwq_ref, cbq_ref)
    k_s[...] = conv(kp_ref[...], cwk_ref, cbk_ref) * (M_DQK ** -0.5)

    ti = lax.broadcasted_iota(jnp.int32, (M_CHUNK, M_CHUNK), 0)
    si = lax.broadcasted_iota(jnp.int32, (M_CHUNK, M_CHUNK), 1)
    eye = ti == si

    def row2col(r):
        return jnp.sum(jnp.where(eye, r, 0.0), axis=1, keepdims=True)

    def col2row(c):
        return jnp.sum(jnp.where(eye, c, 0.0), axis=0, keepdims=True)

    def run_dir(dirn):
        mask = (si <= ti) if dirn == 0 else (si >= ti)
        c_s[...] = c0_ref[0, 0, dirn, 0]

        def body(it, carry):
            n, m = carry
            c = it if dirn == 0 else nc - 1 - it
            r0 = pl.multiple_of(c * M_CHUNK, M_CHUNK)
            g4 = g_ref[0, c]
            ig = g4[2 * dirn:2 * dirn + 1, :]
            fp = g4[2 * dirn + 1:2 * dirn + 2, :]
            lf = jnp.minimum(fp, 0.0) - jnp.log1p(jnp.exp(-jnp.abs(fp)))
            b_col = jnp.sum(jnp.where(mask, lf, 0.0), axis=1, keepdims=True)
            b_row = col2row(b_col)
            b_end = jnp.sum(lf, axis=1, keepdims=True)
            qc = q_s[pl.ds(r0, M_CHUNK), :]
            kc = k_s[pl.ds(r0, M_CHUNK), :]
            vc = v_ref[pl.ds(r0, M_CHUNK), :]
            d = jnp.where(mask, b_col - b_row + ig, -jnp.inf)
            inter = b_col + m
            m_t = jnp.maximum(inter, jnp.max(d, axis=1, keepdims=True))
            w = jnp.exp(d - m_t) * _mm3(qc, kc, 1, 1)
            a = jnp.exp(inter - m_t)
            c_old = c_s[...]
            num = a * _mm3(qc, c_old) + _mm3(w, vc)
            den = a * jnp.sum(qc * n, axis=1, keepdims=True) + jnp.sum(w, axis=1, keepdims=True)
            h = num / jnp.maximum(jnp.abs(den), jnp.exp(-m_t))
            if dirn == 0:
                h_s[pl.ds(r0, M_CHUNK), :] = h
            else:
                h_s[pl.ds(r0, M_CHUNK), :] += h
            to_end = b_end - b_row + ig
            m_new = jnp.maximum(b_end + m, jnp.max(to_end, axis=1, keepdims=True))
            decay = jnp.exp(b_end + m - m_new)
            kw = kc * row2col(jnp.exp(to_end - m_new))
            c_s[...] = decay * c_old + _mm3(kw.T, vc)
            n_new = decay * n + jnp.sum(kw, axis=0, keepdims=True)
            return n_new, m_new

        n_f, m_f = lax.fori_loop(0, nc, body, (n0_ref[0, 0, dirn, 0], m0_ref[0, 0, dirn, 0]))
        if state_out:
            co_ref[0, 0, dirn, 0] = c_s[...]
            no_ref[0, 0, dirn, 0] = n_f
            mo_ref[0, 0, dirn, 0] = m_f

    run_dir(0)
    run_dir(1)
    hm = h_s[...]
    hn_ref[...] = hm * lax.rsqrt(jnp.mean(hm * hm, axis=-1, keepdims=True) + EPS)


def _mlstm_scan(p_main, gates, conv_w, conv_b, c0, n0, m0, nseq, seq, state_out):
    ntok = nseq * seq
    nc = seq // M_CHUNK
    qblk = M_DQK
    st5 = lambda b, h: (b, 0, 0, h, 0, 0)
    in_specs = [
        pl.BlockSpec((seq, qblk), lambda b, h: (b, h)),
        pl.BlockSpec((seq, qblk), lambda b, h: (b, M_HEADS + h)),
        pl.BlockSpec((seq, M_DV), lambda b, h: (b, M_QK_DIM // M_DV + h)),
        pl.BlockSpec((1, nc, 4, M_CHUNK), lambda b, h: (h, b, 0, 0)),
        pl.BlockSpec((3, qblk), lambda b, h: (0, h)),
        pl.BlockSpec((3, qblk), lambda b, h: (0, M_HEADS + h)),
        pl.BlockSpec((1, qblk), lambda b, h: (0, h)),
        pl.BlockSpec((1, qblk), lambda b, h: (0, M_HEADS + h)),
        pl.BlockSpec((1, 1, 2, 1, M_DQK, M_DV), st5),
        pl.BlockSpec((1, 1, 2, 1, 1, M_DQK), st5),
        pl.BlockSpec((1, 1, 2, 1, 1, 1), st5),
    ]
    out_specs = [pl.BlockSpec((seq, M_DV), lambda b, h: (b, h))]
    out_shape = [jax.ShapeDtypeStruct((ntok, M_V_DIM), F32)]
    if state_out:
        out_specs += [
            pl.BlockSpec((1, 1, 2, 1, M_DQK, M_DV), st5),
            pl.BlockSpec((1, 1, 2, 1, 1, M_DQK), st5),
            pl.BlockSpec((1, 1, 2, 1, 1, 1), st5),
        ]
        out_shape += [
            jax.ShapeDtypeStruct((nseq, 1, 2, M_HEADS, M_DQK, M_DV), F32),
            jax.ShapeDtypeStruct((nseq, 1, 2, M_HEADS, 1, M_DQK), F32),
            jax.ShapeDtypeStruct((nseq, 1, 2, M_HEADS, 1, 1), F32),
        ]
    return pl.pallas_call(
        functools.partial(_scan_kernel, seq, state_out),
        grid=(nseq, M_HEADS),
        in_specs=in_specs,
        out_specs=out_specs,
        out_shape=out_shape,
        scratch_shapes=[
            pltpu.VMEM((seq, M_DQK), F32),
            pltpu.VMEM((seq, M_DQK), F32),
            pltpu.VMEM((seq, M_DV), F32),
            pltpu.VMEM((M_DQK, M_DV), F32),
        ],
        compiler_params=_cparams("arbitrary", "arbitrary"),
        name="mlstm_scan_ctx" if state_out else "mlstm_scan_smp",
    )(p_main, p_main, p_main, gates, conv_w, conv_w, conv_b, conv_b, c0, n0, m0)


def _mout_kernel(x_ref, m_ref, hn_ref, o_ref, hg_ref, w_ref, y_ref):
    t = hn_ref[...] * hg_ref[...] * jax.nn.sigmoid(o_ref[...])
    y_ref[...] = x_ref[...] + m_ref[0, 2:3, :] * _dg(t.astype(BF16), w_ref[...])


def _mlstm_out(x, mods, cmap, hn, p_main, head_g, w_out):
    ntok = x.shape[0]
    tb = PROJ_TB
    return pl.pallas_call(
        _mout_kernel,
        grid=(ntok // tb,),
        in_specs=[
            pl.BlockSpec((tb, D_MODEL), lambda i: (i, 0)),
            pl.BlockSpec((1, 6, D_MODEL), cmap),
            pl.BlockSpec((tb, M_V_DIM), lambda i: (i, 0)),
            pl.BlockSpec((tb, M_V_DIM), lambda i: (i, (M_QK_DIM + M_V_DIM) // M_V_DIM)),
            pl.BlockSpec((1, M_V_DIM), lambda i: (0, 0)),
            pl.BlockSpec((M_V_DIM, D_MODEL), lambda i: (0, 0)),
        ],
        out_specs=pl.BlockSpec((tb, D_MODEL), lambda i: (i, 0)),
        out_shape=jax.ShapeDtypeStruct((ntok, D_MODEL), F32),
        compiler_params=_cparams("arbitrary"),
        name="mlstm_out",
    )(x, mods, hn, p_main, head_g, w_out)


def _aout_kernel(x_ref, m_ref, a_ref, w_ref, y_ref):
    y_ref[...] = x_ref[...] + m_ref[0, 2:3, :] * _dg(a_ref[...].astype(BF16), w_ref[...])


def _attn_out(x, mods, cmap, att, w_out):
    ntok = x.shape[0]
    tb = PROJ_TB
    return pl.pallas_call(
        _aout_kernel,
        grid=(ntok // tb,),
        in_specs=[
            pl.BlockSpec((tb, D_MODEL), lambda i: (i, 0)),
            pl.BlockSpec((1, 6, D_MODEL), cmap),
            pl.BlockSpec((tb, D_MODEL), lambda i: (i, 0)),
            pl.BlockSpec((D_MODEL, D_MODEL), lambda i: (0, 0)),
        ],
        out_specs=pl.BlockSpec((tb, D_MODEL), lambda i: (i, 0)),
        out_shape=jax.ShapeDtypeStruct((ntok, D_MODEL), F32),
        compiler_params=_cparams("arbitrary"),
        name="attn_out",
    )(x, mods, att, w_out)


def _aproj_kernel(rope, cache_out, x_ref, m_ref, g_ref, w_ref, gs_ref, qg_ref, kg_ref, *rest):
    rest = list(rest)
    if rope:
        cos_ref, sin_ref = rest[:2]
        rest = rest[2:]
    q_ref, k_ref, v_ref = rest[:3]
    rest = rest[3:]
    h = _modulate(x_ref[...], g_ref[...], m_ref[0, 0:1, :], m_ref[0, 1:2, :])
    p = _dg(h.astype(BF16), w_ref[...])
    nq = A_HEADS * A_HEAD_DIM
    nk = 2 * A_KV_HEADS * A_HEAD_DIM
    gs = gs_ref[...]
    lane = lax.broadcasted_iota(jnp.int32, (1, LANES), 1)
    first_half = (lane % A_ROPE_AXIS) < (A_ROPE_AXIS // 2)

    def norm_rope(xs, gain):
        sq = xs * xs
        s1 = sq.astype(BF16)
        r1 = sq - s1.astype(F32)
        s2 = r1.astype(BF16)
        s3 = (r1 - s2.astype(F32)).astype(BF16)
        ss = _dg(s1, gs) + (_dg(s2, gs) + _dg(s3, gs))
        y = xs * lax.rsqrt(ss * (1.0 / A_HEAD_DIM) + EPS) * gain
        if rope:
            half = A_ROPE_AXIS // 2
            swapped = jnp.where(first_half, pltpu.roll(y, LANES - half, 1), pltpu.roll(y, half, 1))
            y = y * cos_ref[...] + swapped * sin_ref[...]
        return y

    for s in range(nq // LANES):
        q_ref[:, s * LANES:(s + 1) * LANES] = norm_rope(p[:, s * LANES:(s + 1) * LANES], qg_ref[...])
    for s in range(nk // LANES):
        kn = norm_rope(p[:, nq + s * LANES:nq + (s + 1) * LANES], kg_ref[...])
        k_ref[:, s * LANES:(s + 1) * LANES] = kn
        vs = p[:, nq + nk + s * LANES:nq + nk + (s + 1) * LANES]
        v_ref[:, s * LANES:(s + 1) * LANES] = vs
        if cache_out:
            kc_ref, vc_ref = rest
            kc_ref[0, 0, s] = kn[:, :A_HEAD_DIM]
            vc_ref[0, 0, s] = vs[:, :A_HEAD_DIM]


def _attn_project(x, mods, cmap, g, w_qkv2, gsum, qg, kg, rope_tabs, nseq, seq, cache_out):
    ntok = x.shape[0]
    tb = PROJ_TB
    nq = A_HEADS * A_HEAD_DIM
    nk = 2 * A_KV_HEADS * A_HEAD_DIM
    per = seq // tb
    rope = rope_tabs is not None
    in_specs = [
        pl.BlockSpec((tb, D_MODEL), lambda i: (i, 0)),
        pl.BlockSpec((1, 6, D_MODEL), cmap),
        pl.BlockSpec((1, D_MODEL), lambda i: (0, 0)),
        pl.BlockSpec((D_MODEL, nq + 2 * nk), lambda i: (0, 0)),
        pl.BlockSpec((LANES, LANES), lambda i: (0, 0)),
        pl.BlockSpec((1, LANES), lambda i: (0, 0)),
        pl.BlockSpec((1, LANES), lambda i: (0, 0)),
    ]
    args = [x, mods, g, w_qkv2, gsum, qg, kg]
    if rope:
        in_specs += [pl.BlockSpec((tb, LANES), lambda i: (i % per, 0))] * 2
        args += list(rope_tabs)
    out_specs = [
        pl.BlockSpec((tb, nq), lambda i: (i, 0)),
        pl.BlockSpec((tb, nk), lambda i: (i, 0)),
        pl.BlockSpec((tb, nk), lambda i: (i, 0)),
    ]
    out_shape = [
        jax.ShapeDtypeStruct((ntok, nq), F32),
        jax.ShapeDtypeStruct((ntok, nk), F32),
        jax.ShapeDtypeStruct((ntok, nk), F32),
    ]
    if cache_out:
        assert seq == tb
        cspec = pl.BlockSpec((1, 1, A_KV_HEADS, seq, A_HEAD_DIM), lambda i: (i, 0, 0, 0, 0))
        cshape = jax.ShapeDtypeStruct((nseq, 1, A_KV_HEADS, seq, A_HEAD_DIM), F32)
        out_specs += [cspec, cspec]
        out_shape += [cshape, cshape]
    return pl.pallas_call(
        functools.partial(_aproj_kernel, rope, cache_out),
        grid=(ntok // tb,),
        in_specs=in_specs,
        out_specs=out_specs,
        out_shape=out_shape,
        compiler_params=_cparams("arbitrary"),
        name="attn_project_ctx" if cache_out else "attn_project_smp",
    )(*args)


def _attn_kernel(cached, q_ref, k_ref, v_ref, *rest):
    if cached:
        kc_ref, vc_ref, o_ref = rest
    else:
        (o_ref,) = rest
    lane = lax.broadcasted_iota(jnp.int32, (1, LANES), 1)
    q = q_ref[...] * (A_HEAD_DIM ** -0.5)
    k = k_ref[...].astype(BF16)
    v = v_ref[...]
    if cached:
        kc = kc_ref[0, 0].astype(BF16)
        vc = vc_ref[0, 0]
    acc = jnp.zeros(q.shape, F32)
    for e in range(2):
        sel = (lane < A_HEAD_DIM) if e == 0 else (lane >= A_HEAD_DIM)
        qe = jnp.where(sel, q, 0.0).astype(BF16)
        s = _dg(qe, k, 1, 1)
        mx = jnp.max(s, axis=1, keepdims=True)
        if cached:
            sc = _dg(qe, kc, 1, 1)
            mx = jnp.maximum(mx, jnp.max(sc, axis=1, keepdims=True))
        p = jnp.exp(s - mx)
        l = jnp.sum(p, axis=1, keepdims=True)
        o = _dg(p.astype(BF16), jnp.where(sel, v, 0.0).astype(BF16))
        if cached:
            pc = jnp.exp(sc - mx)
            l = l + jnp.sum(pc, axis=1, keepdims=True)
            o = o + _dg(pc.astype(BF16), jnp.where(sel, vc, 0.0).astype(BF16))
        acc = acc + o / l
    o_ref[...] = acc


def _attention(q, k2, v2, cache, nseq, seq):
    ntok = q.shape[0]
    qb = min(ATTN_QB, seq)
    nqb = seq // qb
    npair = A_HEADS // 2
    in_specs = [
        pl.BlockSpec((qb, LANES), lambda b, hp, j: (b * nqb + j, hp)),
        pl.BlockSpec((seq, LANES), lambda b, hp, j: (b, hp // 2)),
        pl.BlockSpec((seq, LANES), lambda b, hp, j: (b, hp // 2)),
    ]
    args = [q, k2, v2]
    if cache is not None:
        past = cache[0].shape[2]
        cspec = pl.BlockSpec((1, 1, past, LANES), lambda b, hp, j: (b, hp // 2, 0, 0))
        in_specs += [cspec, cspec]
        args += list(cache)
    return pl.pallas_call(
        functools.partial(_attn_kernel, cache is not None),
        grid=(nseq, npair, nqb),
        in_specs=in_specs,
        out_specs=pl.BlockSpec((qb, LANES), lambda b, hp, j: (b * nqb + j, hp)),
        out_shape=jax.ShapeDtypeStruct((ntok, A_HEADS * A_HEAD_DIM), F32),
        compiler_params=_cparams("arbitrary", "arbitrary", "arbitrary"),
        name="attention_smp" if cache is not None else "attention_ctx",
    )(*args)


def _batcher_pairs(n):
    pairs = []
    p = 1
    while p < n:
        k = p
        while k >= 1:
            for j in range(k % p, n - k, 2 * k):
                for i in range(min(k, n - j - k)):
                    if (i + j) // (2 * p) == (i + j + k) // (2 * p):
                        pairs.append((i + j, i + j + k))
            k //= 2
        p *= 2
    return pairs


def _sort_levels(levels):
    lv = list(levels)
    for i, j in _batcher_pairs(len(lv)):
        lv[i], lv[j] = jnp.maximum(lv[i], lv[j]), jnp.minimum(lv[i], lv[j])
    return lv


def _pop_top(levels, k, sub):
    lv = list(levels)
    n = len(lv)
    outs = []
    for it in range(k):
        head = lv[0]
        m = jnp.max(head, axis=0, keepdims=True)
        outs.append(m)
        rem = k - 1 - it
        if rem == 0:
            break
        first = jnp.min(jnp.where(head == m, sub, 8.0), axis=0, keepdims=True)
        pop = sub == first
        for q in range(min(n, rem)):
            nxt = lv[q + 1] if q + 1 < n else -jnp.inf
            lv[q] = jnp.where(pop, nxt, lv[q])
    return outs


def _cand_levels(a, b, sub):
    k1 = P_TOPK + 1
    acol = jnp.where(sub == 0.0, a[0], jnp.where(sub == 1.0, a[1], jnp.where(sub == 2.0, a[2], a[3])))
    bcol = jnp.where(sub == 4.0, b[0], jnp.where(sub == 5.0, b[1], b[2]))
    low = sub < 4.0
    levels = []
    for lvl in range(k1):
        na = min(4, k1 // (lvl + 1))
        nb = min(3, k1 // (lvl + 5)) if lvl + 4 < k1 else 0
        valid = sub < float(na)
        x = acol + b[lvl]
        if nb:
            x = jnp.where(low, x, a[lvl + 4] + bcol)
            valid = valid | ((sub >= 4.0) & (sub < float(4 + nb)))
        levels.append(jnp.where(valid, x, -jnp.inf))
    return levels


def _pproj_kernel(x_ref, m_ref, g_ref, wqh_ref, wql_ref, skh_ref, skl_ref,
                  ht_ref, thr_ref, e1_ref, s2_ref):
    h = _modulate(x_ref[...], g_ref[...], m_ref[0, 3:4, :], m_ref[0, 4:5, :])
    ht_ref[...] = h.T.astype(BF16)
    q = _mm3w(h, wqh_ref[...], wql_ref[...])
    k1 = P_TOPK + 1
    sub = lax.broadcasted_iota(jnp.int32, (8, LANES), 0).astype(F32)
    for p in range(P_HEADS):
        sc = []
        for hf in range(2):
            ph = 2 * p + hf
            qh, ql = _split2(q[:, ph * LANES:(ph + 1) * LANES])
            kh = skh_ref[ph]
            sc.append(_dg(kh, qh, 1, 1) + (_dg(kh, ql, 1, 1) + _dg(skl_ref[ph], qh, 1, 1)))
        for lt in range(q.shape[0] // LANES):
            ls = slice(lt * LANES, (lt + 1) * LANES)
            s1 = sc[0][:, ls]
            s2 = sc[1][:, ls]
            a = _pop_top(_sort_levels([s1[8 * r:8 * r + 8] for r in range(P_NKEYS // 8)]), k1, sub)
            b = _pop_top(_sort_levels([s2[8 * r:8 * r + 8] for r in range(P_NKEYS // 8)]), k1, sub)
            v = _pop_top(_cand_levels(a, b, sub), k1, sub)
            tau = 0.5 * (v[P_TOPK - 1] + v[P_TOPK])
            z = jnp.ones_like(tau)
            for kk in range(1, P_TOPK):
                z = z + jnp.exp(v[kk] - v[0])
            thr_ref[p, :, ls] = ((tau - b[0]) - s1) * LOG2E
            e1_ref[p, :, ls] = jnp.exp(s1 - a[0]) * (0.5 / z)
            s2_ref[p, :, ls] = (s2 - b[0]) * LOG2E


def _peer_project(x, mods, cmap, g, wq_hi, wq_lo, sk_hi, sk_lo):
    ntok = x.shape[0]
    tb = PROJ_TB
    nq = P_HEADS * P_DKEY
    sel_spec = pl.BlockSpec((P_HEADS, P_NKEYS, tb), lambda i: (0, 0, i))
    sel_shape = jax.ShapeDtypeStruct((P_HEADS, P_NKEYS, ntok), F32)
    return pl.pallas_call(
        _pproj_kernel,
        grid=(ntok // tb,),
        in_specs=[
            pl.BlockSpec((tb, D_MODEL), lambda i: (i, 0)),
            pl.BlockSpec((1, 6, D_MODEL), cmap),
            pl.BlockSpec((1, D_MODEL), lambda i: (0, 0)),
            pl.BlockSpec((D_MODEL, nq), lambda i: (0, 0)),
            pl.BlockSpec((D_MODEL, nq), lambda i: (0, 0)),
            pl.BlockSpec((2 * P_HEADS, P_NKEYS, P_DKEY // 2), lambda i: (0, 0, 0)),
            pl.BlockSpec((2 * P_HEADS, P_NKEYS, P_DKEY // 2), lambda i: (0, 0, 0)),
        ],
        out_specs=[pl.BlockSpec((D_MODEL, tb), lambda i: (0, i)), sel_spec, sel_spec, sel_spec],
        out_shape=[jax.ShapeDtypeStruct((D_MODEL, ntok), BF16), sel_shape, sel_shape, sel_shape],
        compiler_params=_cparams("arbitrary"),
        name="peer_project",
    )(x, mods, g, wq_hi, wq_lo, sk_hi, sk_lo)


def _pdense_kernel(final, ne, x_ref, m_ref, ht_ref, thr_ref, e1_ref, s2_ref, u_ref, vt_ref, fg_ref,
                   y_ref, act0_s, act1_s, gt0_s, gt1_s, acc_s):
    s = pl.program_id(0)
    nchunk = pl.num_programs(0) - 2
    jc = jnp.clip(s - 2, 0, nchunk - 1) % ne

    @pl.when(s == 0)
    def _():
        act1_s[...] = jnp.zeros(act1_s.shape, F32)
        gt1_s[...] = jnp.zeros(gt1_s.shape, BF16)
        acc_s[...] = jnp.zeros(acc_s.shape, F32)

    @pl.when(s % 2 == 0)
    def _():
        _pdense_step(u_ref, ht_ref, thr_ref, e1_ref, s2_ref, vt_ref, act0_s, act1_s, gt0_s, gt1_s, acc_s)

    @pl.when(s % 2 == 1)
    def _():
        _pdense_step(u_ref, ht_ref, thr_ref, e1_ref, s2_ref, vt_ref, act1_s, act0_s, gt1_s, gt0_s, acc_s)

    @pl.when((s >= 2) & (jc == ne - 1))
    def _():
        xn = x_ref[...] + m_ref[0, 5:6, :] * acc_s[...].T
        if final:
            xn = xn * lax.rsqrt(jnp.mean(xn * xn, axis=-1, keepdims=True) + EPS) * fg_ref[...]
        y_ref[...] = xn
        acc_s[...] = jnp.zeros(acc_s.shape, F32)


def _pdense_step(u_ref, ht_ref, thr_ref, e1_ref, s2_ref, vt_ref, act_w, act_r, gt_w, gt_r, acc_s):
    ec, td = act_w.shape
    na, nb = 4, 32
    half = td // 2

    def half_body(hh, carry):
        h0 = hh * half
        hs = pl.ds(pl.multiple_of(h0, half), half)
        act_w[:, hs] = _dg(u_ref[...], ht_ref[:, hs])
        for lt in range(half // LANES):
            ls = pl.ds(pl.multiple_of(h0 + lt * LANES, LANES), LANES)
            for aq in range(ec // P_NKEYS // na):
                for bq in range(P_NKEYS // nb):
                    bs = slice(bq * nb, (bq + 1) * nb)
                    w = [jnp.zeros((nb, LANES), F32) for _ in range(na)]
                    for p in range(P_HEADS):
                        s2 = s2_ref[p, bs, ls]
                        e2 = jnp.exp2(s2)
                        for ai in range(na):
                            al = aq * na + ai
                            sel = jnp.where(s2 >= thr_ref[p, al:al + 1, ls], e2, 0.0)
                            w[ai] = w[ai] + sel * e1_ref[p, al:al + 1, ls]
                    for ai in range(na):
                        r0 = (aq * na + ai) * P_NKEYS + bq * nb
                        xa = act_r[r0:r0 + nb, ls]
                        gel2 = xa * (1.0 + lax.erf(xa * (2.0 ** -0.5)))
                        gt_w[r0:r0 + nb, ls] = (w[ai] * gel2).astype(BF16)
        acc_s[:, hs] += _dg(vt_ref[...], gt_r[:, hs])
        return carry

    lax.fori_loop(0, td // half, half_body, 0)


def _peer_dense(x, mods, cmap, ht, thr, e1, s2, u, vt, final_g, final):
    ntok = x.shape[0]
    td = PEER_TD
    ec = PEER_EC
    ne = u.shape[0] // ec
    nchunk = (ntok // td) * ne
    ca = lambda s: jnp.minimum(s, nchunk - 1)
    cb = lambda s: jnp.clip(s - 1, 0, nchunk - 1)
    cc = lambda s: jnp.clip(s - 2, 0, nchunk - 1)
    sel_spec = pl.BlockSpec((P_HEADS, P_NKEYS, td), lambda s: (0, 0, cb(s) // ne))
    row_spec = pl.BlockSpec((P_HEADS, ec // P_NKEYS, td), lambda s: (0, cb(s) % ne, cb(s) // ne))
    return pl.pallas_call(
        functools.partial(_pdense_kernel, final, ne),
        grid=(nchunk + 2,),
        in_specs=[
            pl.BlockSpec((td, D_MODEL), lambda s: (cc(s) // ne, 0)),
            pl.BlockSpec((1, 6, D_MODEL), lambda s: cmap(cc(s) // ne)),
            pl.BlockSpec((D_MODEL, td), lambda s: (0, ca(s) // ne)),
            row_spec, row_spec, sel_spec,
            pl.BlockSpec((ec, D_MODEL), lambda s: (ca(s) % ne, 0)),
            pl.BlockSpec((D_MODEL, ec), lambda s: (0, cc(s) % ne)),
            pl.BlockSpec((1, D_MODEL), lambda s: (0, 0)),
        ],
        out_specs=pl.BlockSpec((td, D_MODEL), lambda s: (cc(s) // ne, 0)),
        out_shape=jax.ShapeDtypeStruct((ntok, D_MODEL), F32),
        scratch_shapes=[
            pltpu.VMEM((ec, td), F32),
            pltpu.VMEM((ec, td), F32),
            pltpu.VMEM((ec, td), BF16),
            pltpu.VMEM((ec, td), BF16),
            pltpu.VMEM((D_MODEL, td), F32),
        ],
        compiler_params=_cparams("arbitrary"),
        name="peer_dense",
    )(x, mods, ht, thr, e1, s2, u, vt, final_g)


def _rope_tables(seq):
    n_rows = seq // GRID_W
    rows = jnp.repeat(jnp.arange(n_rows, dtype=F32), GRID_W)
    cols = jnp.tile(jnp.arange(GRID_W, dtype=F32), n_rows)
    half = A_ROPE_AXIS // 2
    inv_freq = ROPE_THETA ** (-jnp.arange(half, dtype=F32) / half)
    ar = rows[:, None] * inv_freq
    ac = cols[:, None] * inv_freq
    cos = jnp.concatenate([jnp.cos(ar), jnp.cos(ar), jnp.cos(ac), jnp.cos(ac)], axis=-1)
    sin = jnp.concatenate([-jnp.sin(ar), jnp.sin(ar), -jnp.sin(ac), jnp.sin(ac)], axis=-1)
    return jnp.tile(cos, (1, LANES // A_HEAD_DIM)), jnp.tile(sin, (1, LANES // A_HEAD_DIM))


def _hi_lo(w):
    hi = w.astype(BF16)
    return hi, (w - hi.astype(F32)).astype(BF16)


def kernel(x_prompt, x_sample, state_mlstm_C, state_mlstm_n, state_mlstm_m, cache_attn_k, cache_attn_v, c, c_ctx, ada_w, ada_b, norm_g, final_g, mlstm_w_in, mlstm_conv_w, mlstm_conv_b, mlstm_gate_b, mlstm_head_g, mlstm_w_out, attn_w_qkv, attn_q_g, attn_k_g, attn_w_out, peer_w_q, peer_subkeys, peer_u, peer_v):
    nb, seq_c, d = x_prompt.shape
    ndb, seq_s, _ = x_sample.shape
    assert d == D_MODEL and seq_c % PROJ_TB == 0 and seq_s % PROJ_TB == 0
    assert (nb * seq_c) % PEER_TD == 0 and seq_s % PEER_TD == 0

    nrow = -(-(1 + ndb) // 8) * 8
    cond = jnp.concatenate([c_ctx[None, :], c, jnp.zeros((nrow - 1 - ndb, d), F32)], axis=0)
    mods_all = _ada_mods(cond, ada_w, ada_b)

    groups = [
        dict(x=x_prompt.reshape(nb * seq_c, d), nseq=nb, seq=seq_c, ctx=True),
        dict(x=x_sample.reshape(ndb * seq_s, d), nseq=ndb, seq=seq_s, ctx=False),
    ]

    def cmap(gr, tb):
        return _cond_map(0, None) if gr["ctx"] else _cond_map(1, gr["seq"] // tb)

    outs = {}
    fg = final_g.reshape(1, d)

    w_in = mlstm_w_in[0]
    w_main = w_in[:, :M_MAIN_DIM].astype(BF16)
    perm = jnp.arange(4 * M_HEADS).reshape(4, M_HEADS).T.reshape(-1)
    wg_t = w_in[:, M_MAIN_DIM:].T[perm]
    wg_hi, wg_lo = _hi_lo(wg_t)
    gate_b = mlstm_gate_b[0][perm].reshape(-1, 1)
    w_mout = mlstm_w_out[0].astype(BF16)
    for gr in groups:
        nseq, seq = gr["nseq"], gr["seq"]
        mods = mods_all[0]
        p_main, gt = _mlstm_project(gr["x"], mods, cmap(gr, PROJ_TB), norm_g[0, 0].reshape(1, d),
                                    w_main, wg_hi, wg_lo, gate_b)
        nc = seq // M_CHUNK
        gates = gt.reshape(M_HEADS, 4, nseq * nc, M_CHUNK).transpose(0, 2, 1, 3)
        if gr["ctx"]:
            c0 = jnp.zeros((nseq, 1, 2, M_HEADS, M_DQK, M_DV), F32)
            n0 = jnp.zeros((nseq, 1, 2, M_HEADS, 1, M_DQK), F32)
            m0 = jnp.zeros((nseq, 1, 2, M_HEADS, 1, 1), F32)
        else:
            c0 = state_mlstm_C.astype(F32)
            n0 = state_mlstm_n.astype(F32).reshape(nseq, -1, 2, M_HEADS, 1, M_DQK)
            m0 = state_mlstm_m.astype(F32).reshape(nseq, -1, 2, M_HEADS, 1, 1)
        res = _mlstm_scan(p_main, gates, mlstm_conv_w[0], mlstm_conv_b[0].reshape(1, -1),
                          c0, n0, m0, nseq, seq, gr["ctx"])
        if gr["ctx"]:
            hn, c_new, n_new, m_new = res
            outs["C"] = c_new
            outs["n"] = n_new.reshape(nseq, 1, 2, M_HEADS, M_DQK)
            outs["m"] = m_new.reshape(nseq, 1, 2, M_HEADS)
        else:
            (hn,) = res
        gr["x"] = _mlstm_out(gr["x"], mods, cmap(gr, PROJ_TB), hn, p_main,
                             mlstm_head_g[0].reshape(1, -1), w_mout)

    def peer(layer, final):
        wq_hi, wq_lo = _hi_lo(peer_w_q[layer])
        sk_hi, sk_lo = _hi_lo(peer_subkeys[layer].reshape(2 * P_HEADS, P_NKEYS, P_DKEY // 2))
        u = peer_u[layer].astype(BF16)
        vt = peer_v[layer].astype(BF16).T
        for gr in groups:
            mods = mods_all[layer]
            ht, thr, e1, s2 = _peer_project(gr["x"], mods, cmap(gr, PROJ_TB), norm_g[layer, 1].reshape(1, d),
                                            wq_hi, wq_lo, sk_hi, sk_lo)
            gr["x"] = _peer_dense(gr["x"], mods, cmap(gr, PEER_TD), ht, thr, e1, s2, u, vt, fg, final)

    peer(0, False)

    wqkv = attn_w_qkv[0]
    nq = A_HEADS * A_HEAD_DIM
    nkv = A_KV_HEADS * A_HEAD_DIM
    dup = lambda w: jnp.tile(w.reshape(d, A_KV_HEADS, 1, A_HEAD_DIM), (1, 1, 2, 1)).reshape(d, 2 * nkv)
    w_qkv2 = jnp.concatenate([wqkv[:, :nq], dup(wqkv[:, nq:nq + nkv]), dup(wqkv[:, nq + nkv:])], axis=1).astype(BF16)
    li = jnp.arange(LANES)
    gsum = (li[:, None] // A_HEAD_DIM == li[None, :] // A_HEAD_DIM).astype(BF16)
    qg = jnp.tile(attn_q_g[0], LANES // A_HEAD_DIM).reshape(1, LANES)
    kg = jnp.tile(attn_k_g[0], LANES // A_HEAD_DIM).reshape(1, LANES)
    w_aout = attn_w_out[0].astype(BF16)
    for gr in groups:
        nseq, seq = gr["nseq"], gr["seq"]
        mods = mods_all[1]
        tabs = None if gr["ctx"] else _rope_tables(seq)
        res = _attn_project(gr["x"], mods, cmap(gr, PROJ_TB), norm_g[1, 0].reshape(1, d), w_qkv2, gsum, qg, kg,
                            tabs, nseq, seq, gr["ctx"])
        if gr["ctx"]:
            q, k2, v2, kc_new, vc_new = res
            outs["k"], outs["v"] = kc_new, vc_new
            cache = None
        else:
            q, k2, v2 = res
            cache = (jnp.tile(cache_attn_k[:, 0].astype(F32), (1, 1, 1, 2)),
                     jnp.tile(cache_attn_v[:, 0].astype(F32), (1, 1, 1, 2)))
        att = _attention(q, k2, v2, cache, nseq, seq)
        gr["x"] = _attn_out(gr["x"], mods, cmap(gr, PROJ_TB), att, w_aout)

    peer(1, True)

    y_prompt = groups[0]["x"].reshape(nb, seq_c, d)
    y_sample = groups[1]["x"].reshape(ndb, seq_s, d)
    return (y_prompt, y_sample, outs["C"], outs["n"], outs["m"], outs["k"], outs["v"])
```

```python
import functools

import jax
import jax.numpy as jnp
from jax import lax
from jax.experimental import pallas as pl
from jax.experimental.pallas import tpu as pltpu

F32 = jnp.float32
BF16 = jnp.bfloat16

EPS = 1e-6
LOG2E = 1.4426950408889634
D_MODEL = 1024
GRID_W = 64
ROPE_THETA = 10000.0

M_HEADS = 4
M_DQK = 128
M_DV = 256
M_CHUNK = 128
M_QK_DIM = 2 * M_HEADS * M_DQK
M_V_DIM = M_HEADS * M_DV
M_MAIN_DIM = M_QK_DIM + 2 * M_V_DIM

A_HEADS = 16
A_KV_HEADS = 4
A_HEAD_DIM = 64
A_ROPE_AXIS = A_HEAD_DIM // 2

P_HEADS = 8
P_NKEYS = 128
P_DKEY = 256
P_TOPK = 16

LANES = 128
PROJ_TB = 256
PEER_TD = 512
PEER_EC = 1024
ATTN_QB = 512
VMEM_LIMIT = 56 * 1024 * 1024


def _cparams(*sem):
    return pltpu.CompilerParams(dimension_semantics=sem, vmem_limit_bytes=VMEM_LIMIT)


def _dg(a, b, ca=1, cb=0):
    return lax.dot_general(a, b, (((ca,), (cb,)), ((), ())), preferred_element_type=F32)


def _split2(x):
    hi = x.astype(BF16)
    lo = (x - hi.astype(F32)).astype(BF16)
    return hi, lo


def _mm3(a, b, ca=1, cb=0):
    ah, al = _split2(a)
    bh, bl = _split2(b)
    return _dg(ah, bh, ca, cb) + (_dg(ah, bl, ca, cb) + _dg(al, bh, ca, cb))


def _mm3w(a, wh, wl, ca=1, cb=0):
    ah, al = _split2(a)
    return _dg(ah, wh, ca, cb) + (_dg(ah, wl, ca, cb) + _dg(al, wh, ca, cb))


def _modulate(x, g, shift, scale):
    y = x * lax.rsqrt(jnp.mean(x * x, axis=-1, keepdims=True) + EPS)
    return (y * g) * (1.0 + scale) + shift


def _cond_map(base, per):
    if per is None:
        return lambda i, *_: (base, 0, 0)
    return lambda i, *_: (base + i // per, 0, 0)


def _ada_kernel(c_ref, w_ref, b_ref, o_ref):
    c = c_ref[...]
    a = c * jax.nn.sigmoid(c)
    o_ref[0] = _mm3(a, w_ref[0]) + b_ref[0]


def _ada_mods(cond, ada_w, ada_b):
    depth, d, n = ada_w.shape
    tn = 512
    rows = cond.shape[0]
    out = pl.pallas_call(
        _ada_kernel,
        grid=(depth, n // tn),
        in_specs=[
            pl.BlockSpec((rows, d), lambda l, j: (0, 0)),
            pl.BlockSpec((1, d, tn), lambda l, j: (l, 0, j)),
            pl.BlockSpec((1, 1, tn), lambda l, j: (l, 0, j)),
        ],
        out_specs=pl.BlockSpec((1, rows, tn), lambda l, j: (l, 0, j)),
        out_shape=jax.ShapeDtypeStruct((depth, rows, n), F32),
        compiler_params=_cparams("arbitrary", "arbitrary"),
        name="ada_mods",
    )(cond, ada_w, ada_b.reshape(depth, 1, n))
    return out.reshape(depth, rows, 6, d)


def _mproj_kernel(x_ref, m_ref, g_ref, w_ref, wgh_ref, wgl_ref, gb_ref, p_ref, gt_ref):
    h = _modulate(x_ref[...], g_ref[...], m_ref[0, 0:1, :], m_ref[0, 1:2, :])
    p_ref[...] = _dg(h.astype(BF16), w_ref[...])
    hh, hl = _split2(h)
    wh = wgh_ref[...]
    gt = _dg(wh, hh, 1, 1) + (_dg(wh, hl, 1, 1) + _dg(wgl_ref[...], hh, 1, 1))
    gt_ref[...] = gt + gb_ref[...]


def _mlstm_project(x, mods, cmap, g, w_main, wg_hi, wg_lo, gate_b):
    ntok = x.shape[0]
    tb = PROJ_TB
    ng = wg_hi.shape[0]
    return pl.pallas_call(
        _mproj_kernel,
        grid=(ntok // tb,),
        in_specs=[
            pl.BlockSpec((tb, D_MODEL), lambda i: (i, 0)),
            pl.BlockSpec((1, 6, D_MODEL), cmap),
            pl.BlockSpec((1, D_MODEL), lambda i: (0, 0)),
            pl.BlockSpec((D_MODEL, M_MAIN_DIM), lambda i: (0, 0)),
            pl.BlockSpec((ng, D_MODEL), lambda i: (0, 0)),
            pl.BlockSpec((ng, D_MODEL), lambda i: (0, 0)),
            pl.BlockSpec((ng, 1), lambda i: (0, 0)),
        ],
        out_specs=[
            pl.BlockSpec((tb, M_MAIN_DIM), lambda i: (i, 0)),
            pl.BlockSpec((ng, tb), lambda i: (0, i)),
        ],
        out_shape=[
            jax.ShapeDtypeStruct((ntok, M_MAIN_DIM), F32),
            jax.ShapeDtypeStruct((ng, ntok), F32),
        ],
        compiler_params=_cparams("arbitrary"),
        name="mlstm_project",
    )(x, mods, g, w_main, wg_hi, wg_lo, gate_b)


def _scan_kernel(seq, state_out, qp_ref, kp_ref, v_ref, g_ref, cwq_ref, cwk_ref, cbq_ref, cbk_ref,
                 c0_ref, n0_ref, m0_ref, hn_ref, *rest):
    if state_out:
        co_ref, no_ref, mo_ref, q_s, k_s, hf_s, hb_s, cf_s, cb_s = rest
    else:
        q_s, k_s, hf_s, hb_s, cf_s, cb_s = rest
    h_dir = (hf_s, hb_s)
    c_dir = (cf_s, cb_s)
    nc = seq // M_CHUNK
    rows = lax.broadcasted_iota(jnp.int32, (seq, 1), 0)

    def conv(x, w_ref, b_ref):
        xm1 = jnp.where(rows == 0, 0.0, pltpu.roll(x, 1, 0))
        xp1 = jnp.where(rows == seq - 1, 0.0, pltpu.roll(x, seq - 1, 0))
        y = xm1 * w_ref[0:1, :] + x * w_ref[1:2, :] + xp1 * w_ref[2:3, :] + b_ref[...]
        return y * jax.nn.sigmoid(y)

    q_s[...] = conv(qp_ref[...], cwq_ref, cbq_ref)
    k_s[...] = conv(kp_ref[...], cwk_ref, cbk_ref) * (M_DQK ** -0.5)

    ti = lax.broadcasted_iota(jnp.int32, (M_CHUNK, M_CHUNK), 0)
    si = lax.broadcasted_iota(jnp.int32, (M_CHUNK, M_CHUNK), 1)
    eye = ti == si

    def row2col(r):
        return jnp.sum(jnp.where(eye, r, 0.0), axis=1, keepdims=True)

    def col2row(c):
        return jnp.sum(jnp.where(eye, c, 0.0), axis=0, keepdims=True)

    def chunk_step(dirn, c, n, m):
        mask = (si <= ti) if dirn == 0 else (si >= ti)
        c_s = c_dir[dirn]
        r0 = pl.multiple_of(c * M_CHUNK, M_CHUNK)
        g4 = g_ref[0, c]
        ig = g4[2 * dirn:2 * dirn + 1, :]
        fp = g4[2 * dirn + 1:2 * dirn + 2, :]
        lf = jnp.minimum(fp, 0.0) - jnp.log1p(jnp.exp(-jnp.abs(fp)))
        b_col = jnp.sum(jnp.where(mask, lf, 0.0), axis=1, keepdims=True)
        b_row = col2row(b_col)
        b_end = jnp.sum(lf, axis=1, keepdims=True)
        qc = q_s[pl.ds(r0, M_CHUNK), :]
        kc = k_s[pl.ds(r0, M_CHUNK), :]
        vc = v_ref[pl.ds(r0, M_CHUNK), :]
        d = jnp.where(mask, b_col - b_row + ig, -jnp.inf)
        inter = b_col + m
        m_t = jnp.maximum(inter, jnp.max(d, axis=1, keepdims=True))
        w = jnp.exp(d - m_t) * _mm3(qc, kc, 1, 1)
        a = jnp.exp(inter - m_t)
        c_old = c_s[...]
        num = a * _mm3(qc, c_old) + _mm3(w, vc)
        den = a * jnp.sum(qc * n, axis=1, keepdims=True) + jnp.sum(w, axis=1, keepdims=True)
        h_dir[dirn][pl.ds(r0, M_CHUNK), :] = num / jnp.maximum(jnp.abs(den), jnp.exp(-m_t))
        to_end = b_end - b_row + ig
        m_new = jnp.maximum(b_end + m, jnp.max(to_end, axis=1, keepdims=True))
        decay = jnp.exp(b_end + m - m_new)
        kw = kc * row2col(jnp.exp(to_end - m_new))
        c_s[...] = decay * c_old + _mm3(kw.T, vc)
        n_new = decay * n + jnp.sum(kw, axis=0, keepdims=True)
        return n_new, m_new

    for dirn in range(2):
        c_dir[dirn][...] = c0_ref[0, 0, dirn, 0]

    def body(it, carry):
        nf, mf, nb, mb = carry
        nf, mf = chunk_step(0, it, nf, mf)
        nb, mb = chunk_step(1, nc - 1 - it, nb, mb)
        return nf, mf, nb, mb

    init = (n0_ref[0, 0, 0, 0], m0_ref[0, 0, 0, 0], n0_ref[0, 0, 1, 0], m0_ref[0, 0, 1, 0])
    fin = lax.fori_loop(0, nc, body, init)
    if state_out:
        for dirn in range(2):
            co_ref[0, 0, dirn, 0] = c_dir[dirn][...]
            no_ref[0, 0, dirn, 0] = fin[2 * dirn]
            mo_ref[0, 0, dirn, 0] = fin[2 * dirn + 1]

    hm = hf_s[...] + hb_s[...]
    hn_ref[...] = hm * lax.rsqrt(jnp.mean(hm * hm, axis=-1, keepdims=True) + EPS)


def _mlstm_scan(p_main, gates, conv_w, conv_b, c0, n0, m0, nseq, seq, state_out):
    ntok = nseq * seq
    nc = seq // M_CHUNK
    qblk = M_DQK
    st5 = lambda b, h: (b, 0, 0, h, 0, 0)
    in_specs = [
        pl.BlockSpec((seq, qblk), lambda b, h: (b, h)),
        pl.BlockSpec((seq, qblk), lambda b, h: (b, M_HEADS + h)),
        pl.BlockSpec((seq, M_DV), lambda b, h: (b, M_QK_DIM // M_DV + h)),
        pl.BlockSpec((1, nc, 4, M_CHUNK), lambda b, h: (h, b, 0, 0)),
        pl.BlockSpec((3, qblk), lambda b, h: (0, h)),
        pl.BlockSpec((3, qblk), lambda b, h: (0, M_HEADS + h)),
        pl.BlockSpec((1, qblk), lambda b, h: (0, h)),
        pl.BlockSpec((1, qblk), lambda b, h: (0, M_HEADS + h)),
        pl.BlockSpec((1, 1, 2, 1, M_DQK, M_DV), st5),
        pl.BlockSpec((1, 1, 2, 1, 1, M_DQK), st5),
        pl.BlockSpec((1, 1, 2, 1, 1, 1), st5),
    ]
    out_specs = [pl.BlockSpec((seq, M_DV), lambda b, h: (b, h))]
    out_shape = [jax.ShapeDtypeStruct((ntok, M_V_DIM), F32)]
    if state_out:
        out_specs += [
            pl.BlockSpec((1, 1, 2, 1, M_DQK, M_DV), st5),
            pl.BlockSpec((1, 1, 2, 1, 1, M_DQK), st5),
            pl.BlockSpec((1, 1, 2, 1, 1, 1), st5),
        ]
        out_shape += [
            jax.ShapeDtypeStruct((nseq, 1, 2, M_HEADS, M_DQK, M_DV), F32),
            jax.ShapeDtypeStruct((nseq, 1, 2, M_HEADS, 1, M_DQK), F32),
            jax.ShapeDtypeStruct((nseq, 1, 2, M_HEADS, 1, 1), F32),
        ]
    return pl.pallas_call(
        functools.partial(_scan_kernel, seq, state_out),
        grid=(nseq, M_HEADS),
        in_specs=in_specs,
        out_specs=out_specs,
        out_shape=out_shape,
        scratch_shapes=[
            pltpu.VMEM((seq, M_DQK), F32),
            pltpu.VMEM((seq, M_DQK), F32),
            pltpu.VMEM((seq, M_DV), F32),
            pltpu.VMEM((seq, M_DV), F32),
            pltpu.VMEM((M_DQK, M_DV), F32),
            pltpu.VMEM((M_DQK, M_DV), F32),
        ],
        compiler_params=_cparams("arbitrary", "arbitrary"),
        name="mlstm_scan_ctx" if state_out else "mlstm_scan_smp",
    )(p_main, p_main, p_main, gates, conv_w, conv_w, conv_b, conv_b, c0, n0, m0)


def _mout_kernel(x_ref, m_ref, hn_ref, o_ref, hg_ref, w_ref, y_ref):
    t = hn_ref[...] * hg_ref[...] * jax.nn.sigmoid(o_ref[...])
    y_ref[...] = x_ref[...] + m_ref[0, 2:3, :] * _dg(t.astype(BF16), w_ref[...])


def _mlstm_out(x, mods, cmap, hn, p_main, head_g, w_out):
    ntok = x.shape[0]
    tb = PROJ_TB
    return pl.pallas_call(
        _mout_kernel,
        grid=(ntok // tb,),
        in_specs=[
            pl.BlockSpec((tb, D_MODEL), lambda i: (i, 0)),
            pl.BlockSpec((1, 6, D_MODEL), cmap),
            pl.BlockSpec((tb, M_V_DIM), lambda i: (i, 0)),
            pl.BlockSpec((tb, M_V_DIM), lambda i: (i, (M_QK_DIM + M_V_DIM) // M_V_DIM)),
            pl.BlockSpec((1, M_V_DIM), lambda i: (0, 0)),
            pl.BlockSpec((M_V_DIM, D_MODEL), lambda i: (0, 0)),
        ],
        out_specs=pl.BlockSpec((tb, D_MODEL), lambda i: (i, 0)),
        out_shape=jax.ShapeDtypeStruct((ntok, D_MODEL), F32),
        compiler_params=_cparams("arbitrary"),
        name="mlstm_out",
    )(x, mods, hn, p_main, head_g, w_out)


def _aout_kernel(x_ref, m_ref, a_ref, w_ref, y_ref):
    y_ref[...] = x_ref[...] + m_ref[0, 2:3, :] * _dg(a_ref[...].astype(BF16), w_ref[...])


def _attn_out(x, mods, cmap, att, w_out):
    ntok = x.shape[0]
    tb = PROJ_TB
    return pl.pallas_call(
        _aout_kernel,
        grid=(ntok // tb,),
        in_specs=[
            pl.BlockSpec((tb, D_MODEL), lambda i: (i, 0)),
            pl.BlockSpec((1, 6, D_MODEL), cmap),
            pl.BlockSpec((tb, D_MODEL), lambda i: (i, 0)),
            pl.BlockSpec((D_MODEL, D_MODEL), lambda i: (0, 0)),
        ],
        out_specs=pl.BlockSpec((tb, D_MODEL), lambda i: (i, 0)),
        out_shape=jax.ShapeDtypeStruct((ntok, D_MODEL), F32),
        compiler_params=_cparams("arbitrary"),
        name="attn_out",
    )(x, mods, att, w_out)


def _aproj_kernel(rope, cache_out, x_ref, m_ref, g_ref, w_ref, gs_ref, qg_ref, kg_ref, *rest):
    rest = list(rest)
    if rope:
        cos_ref, sin_ref = rest[:2]
        rest = rest[2:]
    q_ref, k_ref, v_ref = rest[:3]
    rest = rest[3:]
    h = _modulate(x_ref[...], g_ref[...], m_ref[0, 0:1, :], m_ref[0, 1:2, :])
    p = _dg(h.astype(BF16), w_ref[...])
    nq = A_HEADS * A_HEAD_DIM
    nk = 2 * A_KV_HEADS * A_HEAD_DIM
    gs = gs_ref[...]
    lane = lax.broadcasted_iota(jnp.int32, (1, LANES), 1)
    first_half = (lane % A_ROPE_AXIS) < (A_ROPE_AXIS // 2)

    def norm_rope(xs, gain):
        sq = xs * xs
        s1 = sq.astype(BF16)
        r1 = sq - s1.astype(F32)
        s2 = r1.astype(BF16)
        s3 = (r1 - s2.astype(F32)).astype(BF16)
        ss = _dg(s1, gs) + (_dg(s2, gs) + _dg(s3, gs))
        y = xs * lax.rsqrt(ss * (1.0 / A_HEAD_DIM) + EPS) * gain
        if rope:
            half = A_ROPE_AXIS // 2
            swapped = jnp.where(first_half, pltpu.roll(y, LANES - half, 1), pltpu.roll(y, half, 1))
            y = y * cos_ref[...] + swapped * sin_ref[...]
        return y

    for s in range(nq // LANES):
        q_ref[:, s * LANES:(s + 1) * LANES] = norm_rope(p[:, s * LANES:(s + 1) * LANES], qg_ref[...])
    for s in range(nk // LANES):
        kn = norm_rope(p[:, nq + s * LANES:nq + (s + 1) * LANES], kg_ref[...])
        k_ref[:, s * LANES:(s + 1) * LANES] = kn
        vs = p[:, nq + nk + s * LANES:nq + nk + (s + 1) * LANES]
        v_ref[:, s * LANES:(s + 1) * LANES] = vs
        if cache_out:
            kc_ref, vc_ref = rest
            kc_ref[0, 0, s] = kn[:, :A_HEAD_DIM]
            vc_ref[0, 0, s] = vs[:, :A_HEAD_DIM]


def _attn_project(x, mods, cmap, g, w_qkv2, gsum, qg, kg, rope_tabs, nseq, seq, cache_out):
    ntok = x.shape[0]
    tb = PROJ_TB
    nq = A_HEADS * A_HEAD_DIM
    nk = 2 * A_KV_HEADS * A_HEAD_DIM
    per = seq // tb
    rope = rope_tabs is not None
    in_specs = [
        pl.BlockSpec((tb, D_MODEL), lambda i: (i, 0)),
        pl.BlockSpec((1, 6, D_MODEL), cmap),
        pl.BlockSpec((1, D_MODEL), lambda i: (0, 0)),
        pl.BlockSpec((D_MODEL, nq + 2 * nk), lambda i: (0, 0)),
        pl.BlockSpec((LANES, LANES), lambda i: (0, 0)),
        pl.BlockSpec((1, LANES), lambda i: (0, 0)),
        pl.BlockSpec((1, LANES), lambda i: (0, 0)),
    ]
    args = [x, mods, g, w_qkv2, gsum, qg, kg]
    if rope:
        in_specs += [pl.BlockSpec((tb, LANES), lambda i: (i % per, 0))] * 2
        args += list(rope_tabs)
    out_specs = [
        pl.BlockSpec((tb, nq), lambda i: (i, 0)),
        pl.BlockSpec((tb, nk), lambda i: (i, 0)),
        pl.BlockSpec((tb, nk), lambda i: (i, 0)),
    ]
    out_shape = [
        jax.ShapeDtypeStruct((ntok, nq), F32),
        jax.ShapeDtypeStruct((ntok, nk), F32),
        jax.ShapeDtypeStruct((ntok, nk), F32),
    ]
    if cache_out:
        assert seq == tb
        cspec = pl.BlockSpec((1, 1, A_KV_HEADS, seq, A_HEAD_DIM), lambda i: (i, 0, 0, 0, 0))
        cshape = jax.ShapeDtypeStruct((nseq, 1, A_KV_HEADS, seq, A_HEAD_DIM), F32)
        out_specs += [cspec, cspec]
        out_shape += [cshape, cshape]
    return pl.pallas_call(
        functools.partial(_aproj_kernel, rope, cache_out),
        grid=(ntok // tb,),
        in_specs=in_specs,
        out_specs=out_specs,
        out_shape=out_shape,
        compiler_params=_cparams("arbitrary"),
        name="attn_project_ctx" if cache_out else "attn_project_smp",
    )(*args)


def _attn_kernel(cached, q_ref, k_ref, v_ref, *rest):
    if cached:
        kc_ref, vc_ref, o_ref = rest
    else:
        (o_ref,) = rest
    lane = lax.broadcasted_iota(jnp.int32, (1, LANES), 1)
    q = q_ref[...] * (A_HEAD_DIM ** -0.5)
    k = k_ref[...].astype(BF16)
    v = v_ref[...]
    if cached:
        kc = kc_ref[0, 0].astype(BF16)
        vc = vc_ref[0, 0]
    acc = jnp.zeros(q.shape, F32)
    for e in range(2):
        sel = (lane < A_HEAD_DIM) if e == 0 else (lane >= A_HEAD_DIM)
        qe = jnp.where(sel, q, 0.0).astype(BF16)
        s = _dg(qe, k, 1, 1)
        mx = jnp.max(s, axis=1, keepdims=True)
        if cached:
            sc = _dg(qe, kc, 1, 1)
            mx = jnp.maximum(mx, jnp.max(sc, axis=1, keepdims=True))
        p = jnp.exp(s - mx)
        l = jnp.sum(p, axis=1, keepdims=True)
        o = _dg(p.astype(BF16), jnp.where(sel, v, 0.0).astype(BF16))
        if cached:
            pc = jnp.exp(sc - mx)
            l = l + jnp.sum(pc, axis=1, keepdims=True)
            o = o + _dg(pc.astype(BF16), jnp.where(sel, vc, 0.0).astype(BF16))
        acc = acc + o / l
    o_ref[...] = acc


def _attention(q, k2, v2, cache, nseq, seq):
    ntok = q.shape[0]
    qb = min(ATTN_QB, seq)
    nqb = seq // qb
    npair = A_HEADS // 2
    in_specs = [
        pl.BlockSpec((qb, LANES), lambda b, hp, j: (b * nqb + j, hp)),
        pl.BlockSpec((seq, LANES), lambda b, hp, j: (b, hp // 2)),
        pl.BlockSpec((seq, LANES), lambda b, hp, j: (b, hp // 2)),
    ]
    args = [q, k2, v2]
    if cache is not None:
        past = cache[0].shape[2]
        cspec = pl.BlockSpec((1, 1, past, LANES), lambda b, hp, j: (b, hp // 2, 0, 0))
        in_specs += [cspec, cspec]
        args += list(cache)
    return pl.pallas_call(
        functools.partial(_attn_kernel, cache is not None),
        grid=(nseq, npair, nqb),
        in_specs=in_specs,
        out_specs=pl.BlockSpec((qb, LANES), lambda b, hp, j: (b * nqb + j, hp)),
        out_shape=jax.ShapeDtypeStruct((ntok, A_HEADS * A_HEAD_DIM), F32),
        compiler_params=_cparams("arbitrary", "arbitrary", "arbitrary"),
        name="attention_smp" if cache is not None else "attention_ctx",
    )(*args)


def _batcher_pairs(n):
    pairs = []
    p = 1
    while p < n:
        k = p
        while k >= 1:
            for j in range(k % p, n - k, 2 * k):
                for i in range(min(k, n - j - k)):
                    if (i + j) // (2 * p) == (i + j + k) // (2 * p):
                        pairs.append((i + j, i + j + k))
            k //= 2
        p *= 2
    return pairs


def _sort_levels(levels):
    lv = list(levels)
    for i, j in _batcher_pairs(len(lv)):
        lv[i], lv[j] = jnp.maximum(lv[i], lv[j]), jnp.minimum(lv[i], lv[j])
    return lv


def _pop_top(levels, k, sub):
    lv = list(levels)
    n = len(lv)
    outs = []
    for it in range(k):
        head = lv[0]
        m = jnp.max(head, axis=0, keepdims=True)
        outs.append(m)
        rem = k - 1 - it
        if rem == 0:
            break
        first = jnp.min(jnp.where(head == m, sub, 8.0), axis=0, keepdims=True)
        pop = sub == first
        for q in range(min(n, rem)):
            nxt = lv[q + 1] if q + 1 < n else -jnp.inf
            lv[q] = jnp.where(pop, nxt, lv[q])
    return outs


def _cand_levels(a, b, sub):
    k1 = P_TOPK + 1
    acol = jnp.where(sub == 0.0, a[0], jnp.where(sub == 1.0, a[1], jnp.where(sub == 2.0, a[2], a[3])))
    bcol = jnp.where(sub == 4.0, b[0], jnp.where(sub == 5.0, b[1], b[2]))
    low = sub < 4.0
    levels = []
    for lvl in range(k1):
        na = min(4, k1 // (lvl + 1))
        nb = min(3, k1 // (lvl + 5)) if lvl + 4 < k1 else 0
        valid = sub < float(na)
        x = acol + b[lvl]
        if nb:
            x = jnp.where(low, x, a[lvl + 4] + bcol)
            valid = valid | ((sub >= 4.0) & (sub < float(4 + nb)))
        levels.append(jnp.where(valid, x, -jnp.inf))
    return levels


def _pproj_kernel(x_ref, m_ref, g_ref, wqh_ref, wql_ref, skh_ref, skl_ref,
                  ht_ref, thr_ref, e1_ref, s2_ref):
    h = _modulate(x_ref[...], g_ref[...], m_ref[0, 3:4, :], m_ref[0, 4:5, :])
    ht_ref[...] = h.T.astype(BF16)
    q = _mm3w(h, wqh_ref[...], wql_ref[...])
    k1 = P_TOPK + 1
    sub = lax.broadcasted_iota(jnp.int32, (8, LANES), 0).astype(F32)
    for p in range(P_HEADS):
        sc = []
        for hf in range(2):
            ph = 2 * p + hf
            qh, ql = _split2(q[:, ph * LANES:(ph + 1) * LANES])
            kh = skh_ref[ph]
            sc.append(_dg(kh, qh, 1, 1) + (_dg(kh, ql, 1, 1) + _dg(skl_ref[ph], qh, 1, 1)))
        for lt in range(q.shape[0] // LANES):
            ls = slice(lt * LANES, (lt + 1) * LANES)
            s1 = sc[0][:, ls]
            s2 = sc[1][:, ls]
            a = _pop_top(_sort_levels([s1[8 * r:8 * r + 8] for r in range(P_NKEYS // 8)]), k1, sub)
            b = _pop_top(_sort_levels([s2[8 * r:8 * r + 8] for r in range(P_NKEYS // 8)]), k1, sub)
            v = _pop_top(_cand_levels(a, b, sub), k1, sub)
            tau = 0.5 * (v[P_TOPK - 1] + v[P_TOPK])
            z = jnp.ones_like(tau)
            for kk in range(1, P_TOPK):
                z = z + jnp.exp(v[kk] - v[0])
            thr_ref[p, :, ls] = ((tau - b[0]) - s1) * LOG2E
            e1_ref[p, :, ls] = jnp.exp(s1 - a[0]) * (0.5 / z)
            s2_ref[p, :, ls] = (s2 - b[0]) * LOG2E


def _peer_project(x, mods, cmap, g, wq_hi, wq_lo, sk_hi, sk_lo):
    ntok = x.shape[0]
    tb = PROJ_TB
    nq = P_HEADS * P_DKEY
    sel_spec = pl.BlockSpec((P_HEADS, P_NKEYS, tb), lambda i: (0, 0, i))
    sel_shape = jax.ShapeDtypeStruct((P_HEADS, P_NKEYS, ntok), F32)
    return pl.pallas_call(
        _pproj_kernel,
        grid=(ntok // tb,),
        in_specs=[
            pl.BlockSpec((tb, D_MODEL), lambda i: (i, 0)),
            pl.BlockSpec((1, 6, D_MODEL), cmap),
            pl.BlockSpec((1, D_MODEL), lambda i: (0, 0)),
            pl.BlockSpec((D_MODEL, nq), lambda i: (0, 0)),
            pl.BlockSpec((D_MODEL, nq), lambda i: (0, 0)),
            pl.BlockSpec((2 * P_HEADS, P_NKEYS, P_DKEY // 2), lambda i: (0, 0, 0)),
            pl.BlockSpec((2 * P_HEADS, P_NKEYS, P_DKEY // 2), lambda i: (0, 0, 0)),
        ],
        out_specs=[pl.BlockSpec((D_MODEL, tb), lambda i: (0, i)), sel_spec, sel_spec, sel_spec],
        out_shape=[jax.ShapeDtypeStruct((D_MODEL, ntok), BF16), sel_shape, sel_shape, sel_shape],
        compiler_params=_cparams("arbitrary"),
        name="peer_project",
    )(x, mods, g, wq_hi, wq_lo, sk_hi, sk_lo)


def _pdense_kernel(final, ne, x_ref, m_ref, ht_ref, thr_ref, e1_ref, s2_ref, u_ref, vt_ref, fg_ref,
                   y_ref, act0_s, act1_s, gt0_s, gt1_s, acc_s):
    s = pl.program_id(0)
    nchunk = pl.num_programs(0) - 2
    jc = jnp.clip(s - 2, 0, nchunk - 1) % ne

    @pl.when(s == 0)
    def _():
        act1_s[...] = jnp.zeros(act1_s.shape, F32)
        gt1_s[...] = jnp.zeros(gt1_s.shape, BF16)
        acc_s[...] = jnp.zeros(acc_s.shape, F32)

    @pl.when(s % 2 == 0)
    def _():
        _pdense_step(u_ref, ht_ref, thr_ref, e1_ref, s2_ref, vt_ref, act0_s, act1_s, gt0_s, gt1_s, acc_s)

    @pl.when(s % 2 == 1)
    def _():
        _pdense_step(u_ref, ht_ref, thr_ref, e1_ref, s2_ref, vt_ref, act1_s, act0_s, gt1_s, gt0_s, acc_s)

    @pl.when((s >= 2) & (jc == ne - 1))
    def _():
        xn = x_ref[...] + m_ref[0, 5:6, :] * acc_s[...].T
        if final:
            xn = xn * lax.rsqrt(jnp.mean(xn * xn, axis=-1, keepdims=True) + EPS) * fg_ref[...]
        y_ref[...] = xn
        acc_s[...] = jnp.zeros(acc_s.shape, F32)


def _pdense_step(u_ref, ht_ref, thr_ref, e1_ref, s2_ref, vt_ref, act_w, act_r, gt_w, gt_r, acc_s):
    ec, td = act_w.shape
    na, nb = 4, 32
    half = td // 2
    blk_per_piece = 2

    def half_body(hh, carry):
        h0 = hh * half
        hs = pl.ds(pl.multiple_of(h0, half), half)
        nblk = (half // LANES) * (ec // P_NKEYS // na) * (P_NKEYS // nb)
        npiece = nblk // blk_per_piece
        ra = ec // npiece
        rc = acc_s.shape[0] // npiece
        tl = pl.ds(pl.multiple_of(h0, LANES), LANES)
        blk = 0
        for lt in range(half // LANES):
            ls = pl.ds(pl.multiple_of(h0 + lt * LANES, LANES), LANES)
            for aq in range(ec // P_NKEYS // na):
                for bq in range(P_NKEYS // nb):
                    if blk % blk_per_piece == 0:
                        pc = blk // blk_per_piece
                        act_w[pc * ra:(pc + 1) * ra, hs] = _dg(u_ref[pc * ra:(pc + 1) * ra, :], ht_ref[:, hs])
                        acc_s[pc * rc:(pc + 1) * rc, hs] += _dg(vt_ref[pc * rc:(pc + 1) * rc, :], gt_r[:, hs])
                        ta = act_w[(pc + 1) * ra - 8:(pc + 1) * ra, tl]
                        tc = acc_s[(pc + 1) * rc - 8:(pc + 1) * rc, tl]
                        zero = jnp.where((ta != ta) | (tc != tc), 1.0, 0.0)
                        zero = jnp.concatenate([zero] * (nb // 8), axis=0)
                    blk += 1
                    bs = slice(bq * nb, (bq + 1) * nb)
                    w = [zero for _ in range(na)]
                    for p in range(P_HEADS):
                        s2 = s2_ref[p, bs, ls]
                        e2 = jnp.exp2(s2)
                        for ai in range(na):
                            al = aq * na + ai
                            sel = jnp.where(s2 >= thr_ref[p, al:al + 1, ls], e2, 0.0)
                            w[ai] = w[ai] + sel * e1_ref[p, al:al + 1, ls]
                    for ai in range(na):
                        r0 = (aq * na + ai) * P_NKEYS + bq * nb
                        xa = act_r[r0:r0 + nb, ls]
                        gel2 = xa * (1.0 + lax.erf(xa * (2.0 ** -0.5)))
                        gt_w[r0:r0 + nb, ls] = (w[ai] * gel2).astype(BF16)
        return carry

    lax.fori_loop(0, td // half, half_body, 0)


def _peer_dense(x, mods, cmap, ht, thr, e1, s2, u, vt, layer, final_g, final):
    ntok = x.shape[0]
    td = PEER_TD
    ec = PEER_EC
    ne = u.shape[1] // ec
    nchunk = (ntok // td) * ne
    ca = lambda s: jnp.minimum(s, nchunk - 1)
    cb = lambda s: jnp.clip(s - 1, 0, nchunk - 1)
    cc = lambda s: jnp.clip(s - 2, 0, nchunk - 1)
    sel_spec = pl.BlockSpec((P_HEADS, P_NKEYS, td), lambda s: (0, 0, cb(s) // ne))
    row_spec = pl.BlockSpec((P_HEADS, ec // P_NKEYS, td), lambda s: (0, cb(s) % ne, cb(s) // ne))
    return pl.pallas_call(
        functools.partial(_pdense_kernel, final, ne),
        grid=(nchunk + 2,),
        in_specs=[
            pl.BlockSpec((td, D_MODEL), lambda s: (cc(s) // ne, 0)),
            pl.BlockSpec((1, 6, D_MODEL), lambda s: cmap(cc(s) // ne)),
            pl.BlockSpec((D_MODEL, td), lambda s: (0, ca(s) // ne)),
            row_spec, row_spec, sel_spec,
            pl.BlockSpec((None, ec, D_MODEL), lambda s: (layer, ca(s) % ne, 0)),
            pl.BlockSpec((None, D_MODEL, ec), lambda s: (layer, 0, cc(s) % ne)),
            pl.BlockSpec((1, D_MODEL), lambda s: (0, 0)),
        ],
        out_specs=pl.BlockSpec((td, D_MODEL), lambda s: (cc(s) // ne, 0)),
        out_shape=jax.ShapeDtypeStruct((ntok, D_MODEL), F32),
        scratch_shapes=[
            pltpu.VMEM((ec, td), F32),
            pltpu.VMEM((ec, td), F32),
            pltpu.VMEM((ec, td), BF16),
            pltpu.VMEM((ec, td), BF16),
            pltpu.VMEM((D_MODEL, td), F32),
        ],
        compiler_params=_cparams("arbitrary"),
        name="peer_dense",
    )(x, mods, ht, thr, e1, s2, u, vt, final_g)


def _rope_tables(seq):
    n_rows = seq // GRID_W
    rows = jnp.repeat(jnp.arange(n_rows, dtype=F32), GRID_W)
    cols = jnp.tile(jnp.arange(GRID_W, dtype=F32), n_rows)
    half = A_ROPE_AXIS // 2
    inv_freq = ROPE_THETA ** (-jnp.arange(half, dtype=F32) / half)
    ar = rows[:, None] * inv_freq
    ac = cols[:, None] * inv_freq
    cos = jnp.concatenate([jnp.cos(ar), jnp.cos(ar), jnp.cos(ac), jnp.cos(ac)], axis=-1)
    sin = jnp.concatenate([-jnp.sin(ar), jnp.sin(ar), -jnp.sin(ac), jnp.sin(ac)], axis=-1)
    return jnp.tile(cos, (1, LANES // A_HEAD_DIM)), jnp.tile(sin, (1, LANES // A_HEAD_DIM))


def _hi_lo(w):
    hi = w.astype(BF16)
    return hi, (w - hi.astype(F32)).astype(BF16)


def kernel(x_prompt, x_sample, state_mlstm_C, state_mlstm_n, state_mlstm_m, cache_attn_k, cache_attn_v, c, c_ctx, ada_w, ada_b, norm_g, final_g, mlstm_w_in, mlstm_conv_w, mlstm_conv_b, mlstm_gate_b, mlstm_head_g, mlstm_w_out, attn_w_qkv, attn_q_g, attn_k_g, attn_w_out, peer_w_q, peer_subkeys, peer_u, peer_v):
    nb, seq_c, d = x_prompt.shape
    ndb, seq_s, _ = x_sample.shape
    assert d == D_MODEL and seq_c % PROJ_TB == 0 and seq_s % PROJ_TB == 0
    assert (nb * seq_c) % PEER_TD == 0 and seq_s % PEER_TD == 0

    nrow = -(-(1 + ndb) // 8) * 8
    cond = jnp.concatenate([c_ctx[None, :], c, jnp.zeros((nrow - 1 - ndb, d), F32)], axis=0)
    mods_all = _ada_mods(cond, ada_w, ada_b)

    groups = [
        dict(x=x_prompt.reshape(nb * seq_c, d), nseq=nb, seq=seq_c, ctx=True),
        dict(x=x_sample.reshape(ndb * seq_s, d), nseq=ndb, seq=seq_s, ctx=False),
    ]

    def cmap(gr, tb):
        return _cond_map(0, None) if gr["ctx"] else _cond_map(1, gr["seq"] // tb)

    outs = {}
    fg = final_g.reshape(1, d)

    w_in = mlstm_w_in[0]
    w_main = w_in[:, :M_MAIN_DIM].astype(BF16)
    perm = jnp.arange(4 * M_HEADS).reshape(4, M_HEADS).T.reshape(-1)
    wg_t = w_in[:, M_MAIN_DIM:].T[perm]
    wg_hi, wg_lo = _hi_lo(wg_t)
    gate_b = mlstm_gate_b[0][perm].reshape(-1, 1)
    w_mout = mlstm_w_out[0].astype(BF16)
    for gr in groups:
        nseq, seq = gr["nseq"], gr["seq"]
        mods = mods_all[0]
        p_main, gt = _mlstm_project(gr["x"], mods, cmap(gr, PROJ_TB), norm_g[0, 0].reshape(1, d),
                                    w_main, wg_hi, wg_lo, gate_b)
        nc = seq // M_CHUNK
        gates = gt.reshape(M_HEADS, 4, nseq * nc, M_CHUNK).transpose(0, 2, 1, 3)
        if gr["ctx"]:
            c0 = jnp.zeros((nseq, 1, 2, M_HEADS, M_DQK, M_DV), F32)
            n0 = jnp.zeros((nseq, 1, 2, M_HEADS, 1, M_DQK), F32)
            m0 = jnp.zeros((nseq, 1, 2, M_HEADS, 1, 1), F32)
        else:
            c0 = state_mlstm_C.astype(F32)
            n0 = state_mlstm_n.astype(F32).reshape(nseq, -1, 2, M_HEADS, 1, M_DQK)
            m0 = state_mlstm_m.astype(F32).reshape(nseq, -1, 2, M_HEADS, 1, 1)
        res = _mlstm_scan(p_main, gates, mlstm_conv_w[0], mlstm_conv_b[0].reshape(1, -1),
                          c0, n0, m0, nseq, seq, gr["ctx"])
        if gr["ctx"]:
            hn, c_new, n_new, m_new = res
            outs["C"] = c_new
            outs["n"] = n_new.reshape(nseq, 1, 2, M_HEADS, M_DQK)
            outs["m"] = m_new.reshape(nseq, 1, 2, M_HEADS)
        else:
            (hn,) = res
        gr["x"] = _mlstm_out(gr["x"], mods, cmap(gr, PROJ_TB), hn, p_main,
                             mlstm_head_g[0].reshape(1, -1), w_mout)

    def peer(layer, final):
        wq_hi, wq_lo = _hi_lo(peer_w_q[layer])
        sk_hi, sk_lo = _hi_lo(peer_subkeys[layer].reshape(2 * P_HEADS, P_NKEYS, P_DKEY // 2))
        for gr in groups:
            mods = mods_all[layer]
            ht, thr, e1, s2 = _peer_project(gr["x"], mods, cmap(gr, PROJ_TB), norm_g[layer, 1].reshape(1, d),
                                            wq_hi, wq_lo, sk_hi, sk_lo)
            gr["x"] = _peer_dense(gr["x"], mods, cmap(gr, PEER_TD), ht, thr, e1, s2, u_all, vt_all, layer, fg, final)

    u_all = peer_u.astype(BF16)
    vt_all = jnp.swapaxes(peer_v.astype(BF16), 1, 2)
    peer(0, False)

    wqkv = attn_w_qkv[0]
    nq = A_HEADS * A_HEAD_DIM
    nkv = A_KV_HEADS * A_HEAD_DIM
    dup = lambda w: jnp.tile(w.reshape(d, A_KV_HEADS, 1, A_HEAD_DIM), (1, 1, 2, 1)).reshape(d, 2 * nkv)
    w_qkv2 = jnp.concatenate([wqkv[:, :nq], dup(wqkv[:, nq:nq + nkv]), dup(wqkv[:, nq + nkv:])], axis=1).astype(BF16)
    li = jnp.arange(LANES)
    gsum = (li[:, None] // A_HEAD_DIM == li[None, :] // A_HEAD_DIM).astype(BF16)
    qg = jnp.tile(attn_q_g[0], LANES // A_HEAD_DIM).reshape(1, LANES)
    kg = jnp.tile(attn_k_g[0], LANES // A_HEAD_DIM).reshape(1, LANES)
    w_aout = attn_w_out[0].astype(BF16)
    for gr in groups:
        nseq, seq = gr["nseq"], gr["seq"]
        mods = mods_all[1]
        tabs = None if gr["ctx"] else _rope_tables(seq)
        res = _attn_project(gr["x"], mods, cmap(gr, PROJ_TB), norm_g[1, 0].reshape(1, d), w_qkv2, gsum, qg, kg,
                            tabs, nseq, seq, gr["ctx"])
        if gr["ctx"]:
            q, k2, v2, kc_new, vc_new = res
            outs["k"], outs["v"] = kc_new, vc_new
            cache = None
        else:
            q, k2, v2 = res
            cache = (jnp.tile(cache_attn_k[:, 0].astype(F32), (1, 1, 1, 2)),
                     jnp.tile(cache_attn_v[:, 0].astype(F32), (1, 1, 1, 2)))
        att = _attention(q, k2, v2, cache, nseq, seq)
        gr["x"] = _attn_out(gr["x"], mods, cmap(gr, PROJ_TB), att, w_aout)

    peer(1, True)

    y_prompt = groups[0]["x"].reshape(nb, seq_c, d)
    y_sample = groups[1]["x"].reshape(ndb, seq_s, d)
    return (y_prompt, y_sample, outs["C"], outs["n"], outs["m"], outs["k"], outs["v"])
```

```python
import functools

import jax
import jax.numpy as jnp
from jax import lax
from jax.experimental import pallas as pl
from jax.experimental.pallas import tpu as pltpu

F32 = jnp.float32
BF16 = jnp.bfloat16

EPS = 1e-6
LOG2E = 1.4426950408889634
D_MODEL = 1024
GRID_W = 64
ROPE_THETA = 10000.0

M_HEADS = 4
M_DQK = 128
M_DV = 256
M_CHUNK = 128
M_QK_DIM = 2 * M_HEADS * M_DQK
M_V_DIM = M_HEADS * M_DV
M_MAIN_DIM = M_QK_DIM + 2 * M_V_DIM

A_HEADS = 16
A_KV_HEADS = 4
A_HEAD_DIM = 64
A_ROPE_AXIS = A_HEAD_DIM // 2

P_HEADS = 8
P_NKEYS = 128
P_DKEY = 256
P_TOPK = 16

LANES = 128
PROJ_TB = 256
PEER_TD = 512
PEER_EC = 1024
ATTN_QB = 512
VMEM_LIMIT = 56 * 1024 * 1024


def _cparams(*sem):
    return pltpu.CompilerParams(dimension_semantics=sem, vmem_limit_bytes=VMEM_LIMIT)


def _dg(a, b, ca=1, cb=0):
    return lax.dot_general(a, b, (((ca,), (cb,)), ((), ())), preferred_element_type=F32)


def _split2(x):
    hi = x.astype(BF16)
    lo = (x - hi.astype(F32)).astype(BF16)
    return hi, lo


def _mm3(a, b, ca=1, cb=0):
    ah, al = _split2(a)
    bh, bl = _split2(b)
    return _dg(ah, bh, ca, cb) + (_dg(ah, bl, ca, cb) + _dg(al, bh, ca, cb))


def _mm3w(a, wh, wl, ca=1, cb=0):
    ah, al = _split2(a)
    return _dg(ah, wh, ca, cb) + (_dg(ah, wl, ca, cb) + _dg(al, wh, ca, cb))


def _modulate(x, g, shift, scale):
    y = x * lax.rsqrt(jnp.mean(x * x, axis=-1, keepdims=True) + EPS)
    return (y * g) * (1.0 + scale) + shift


def _cond_map(base, per):
    if per is None:
        return lambda i, *_: (base, 0, 0)
    return lambda i, *_: (base + i // per, 0, 0)


def _ada_kernel(c_ref, w_ref, b_ref, o_ref):
    c = c_ref[...]
    a = c * jax.nn.sigmoid(c)
    o_ref[0] = _mm3(a, w_ref[0]) + b_ref[0]


def _ada_mods(cond, ada_w, ada_b):
    depth, d, n = ada_w.shape
    tn = 512
    rows = cond.shape[0]
    out = pl.pallas_call(
        _ada_kernel,
        grid=(depth, n // tn),
        in_specs=[
            pl.BlockSpec((rows, d), lambda l, j: (0, 0)),
            pl.BlockSpec((1, d, tn), lambda l, j: (l, 0, j)),
            pl.BlockSpec((1, 1, tn), lambda l, j: (l, 0, j)),
        ],
        out_specs=pl.BlockSpec((1, rows, tn), lambda l, j: (l, 0, j)),
        out_shape=jax.ShapeDtypeStruct((depth, rows, n), F32),
        compiler_params=_cparams("arbitrary", "arbitrary"),
        name="ada_mods",
    )(cond, ada_w, ada_b.reshape(depth, 1, n))
    return out.reshape(depth, rows, 6, d)


def _mproj_kernel(x_ref, m_ref, g_ref, w_ref, wgh_ref, wgl_ref, gb_ref, p_ref, gt_ref):
    h = _modulate(x_ref[...], g_ref[...], m_ref[0, 0:1, :], m_ref[0, 1:2, :])
    p_ref[...] = _dg(h.astype(BF16), w_ref[...])
    hh, hl = _split2(h)
    wh = wgh_ref[...]
    gt = _dg(wh, hh, 1, 1) + (_dg(wh, hl, 1, 1) + _dg(wgl_ref[...], hh, 1, 1))
    gt_ref[...] = gt + gb_ref[...]


def _mlstm_project(x, mods, cmap, g, w_main, wg_hi, wg_lo, gate_b):
    ntok = x.shape[0]
    tb = PROJ_TB
    ng = wg_hi.shape[0]
    return pl.pallas_call(
        _mproj_kernel,
        grid=(ntok // tb,),
        in_specs=[
            pl.BlockSpec((tb, D_MODEL), lambda i: (i, 0)),
            pl.BlockSpec((1, 6, D_MODEL), cmap),
            pl.BlockSpec((1, D_MODEL), lambda i: (0, 0)),
            pl.BlockSpec((D_MODEL, M_MAIN_DIM), lambda i: (0, 0)),
            pl.BlockSpec((ng, D_MODEL), lambda i: (0, 0)),
            pl.BlockSpec((ng, D_MODEL), lambda i: (0, 0)),
            pl.BlockSpec((ng, 1), lambda i: (0, 0)),
        ],
        out_specs=[
            pl.BlockSpec((tb, M_MAIN_DIM), lambda i: (i, 0)),
            pl.BlockSpec((ng, tb), lambda i: (0, i)),
        ],
        out_shape=[
            jax.ShapeDtypeStruct((ntok, M_MAIN_DIM), F32),
            jax.ShapeDtypeStruct((ng, ntok), F32),
        ],
        compiler_params=_cparams("arbitrary"),
        name="mlstm_project",
    )(x, mods, g, w_main, wg_hi, wg_lo, gate_b)


def _scan_kernel(seq, state_out, qp_ref, kp_ref, v_ref, g_ref, cwq_ref, cwk_ref, cbq_ref, cbk_ref,
                 c0_ref, n0_ref, m0_ref, hn_ref, *rest):
    if state_out:
        co_ref, no_ref, mo_ref, q_s, k_s, hf_s, hb_s, cf_s, cb_s = rest
    else:
        q_s, k_s, hf_s, hb_s, cf_s, cb_s = rest
    h_dir = (hf_s, hb_s)
    c_dir = (cf_s, cb_s)
    nc = seq // M_CHUNK
    rows = lax.broadcasted_iota(jnp.int32, (seq, 1), 0)

    def conv(x, w_ref, b_ref):
        xm1 = jnp.where(rows == 0, 0.0, pltpu.roll(x, 1, 0))
        xp1 = jnp.where(rows == seq - 1, 0.0, pltpu.roll(x, seq - 1, 0))
        y = xm1 * w_ref[0:1, :] + x * w_ref[1:2, :] + xp1 * w_ref[2:3, :] + b_ref[...]
        return y * jax.nn.sigmoid(y)

    q_s[...] = conv(qp_ref[...], cwq_ref, cbq_ref)
    k_s[...] = conv(kp_ref[...], cwk_ref, cbk_ref) * (M_DQK ** -0.5)

    ti = lax.broadcasted_iota(jnp.int32, (M_CHUNK, M_CHUNK), 0)
    si = lax.broadcasted_iota(jnp.int32, (M_CHUNK, M_CHUNK), 1)
    eye = ti == si

    def row2col(r):
        return jnp.sum(jnp.where(eye, r, 0.0), axis=1, keepdims=True)

    def col2row(c):
        return jnp.sum(jnp.where(eye, c, 0.0), axis=0, keepdims=True)

    def chunk_step(dirn, c, n, m):
        mask = (si <= ti) if dirn == 0 else (si >= ti)
        c_s = c_dir[dirn]
        r0 = pl.multiple_of(c * M_CHUNK, M_CHUNK)
        g4 = g_ref[0, c]
        ig = g4[2 * dirn:2 * dirn + 1, :]
        fp = g4[2 * dirn + 1:2 * dirn + 2, :]
        lf = jnp.minimum(fp, 0.0) - jnp.log1p(jnp.exp(-jnp.abs(fp)))
        b_col = jnp.sum(jnp.where(mask, lf, 0.0), axis=1, keepdims=True)
        b_row = col2row(b_col)
        b_end = jnp.sum(lf, axis=1, keepdims=True)
        qc = q_s[pl.ds(r0, M_CHUNK), :]
        kc = k_s[pl.ds(r0, M_CHUNK), :]
        vc = v_ref[pl.ds(r0, M_CHUNK), :]
        d = jnp.where(mask, b_col - b_row + ig, -jnp.inf)
        inter = b_col + m
        m_t = jnp.maximum(inter, jnp.max(d, axis=1, keepdims=True))
        qb = qc.astype(BF16)
        w = jnp.exp(d - m_t) * _dg(qb, kc.astype(BF16), 1, 1)
        a = jnp.exp(inter - m_t)
        c_old = c_s[...]
        num = a * _dg(qb, c_old.astype(BF16)) + _dg(w.astype(BF16), vc.astype(BF16))
        den = a * jnp.sum(qc * n, axis=1, keepdims=True) + jnp.sum(w, axis=1, keepdims=True)
        h_dir[dirn][pl.ds(r0, M_CHUNK), :] = num / jnp.maximum(jnp.abs(den), jnp.exp(-m_t))
        to_end = b_end - b_row + ig
        m_new = jnp.maximum(b_end + m, jnp.max(to_end, axis=1, keepdims=True))
        decay = jnp.exp(b_end + m - m_new)
        kw = kc * row2col(jnp.exp(to_end - m_new))
        c_s[...] = decay * c_old + _mm3(kw.T, vc)
        n_new = decay * n + jnp.sum(kw, axis=0, keepdims=True)
        return n_new, m_new

    for dirn in range(2):
        c_dir[dirn][...] = c0_ref[0, 0, dirn, 0]

    def body(it, carry):
        nf, mf, nb, mb = carry
        nf, mf = chunk_step(0, it, nf, mf)
        nb, mb = chunk_step(1, nc - 1 - it, nb, mb)
        return nf, mf, nb, mb

    init = (n0_ref[0, 0, 0, 0], m0_ref[0, 0, 0, 0], n0_ref[0, 0, 1, 0], m0_ref[0, 0, 1, 0])
    fin = lax.fori_loop(0, nc, body, init)
    if state_out:
        for dirn in range(2):
            co_ref[0, 0, dirn, 0] = c_dir[dirn][...]
            no_ref[0, 0, dirn, 0] = fin[2 * dirn]
            mo_ref[0, 0, dirn, 0] = fin[2 * dirn + 1]

    hm = hf_s[...] + hb_s[...]
    hn_ref[...] = hm * lax.rsqrt(jnp.mean(hm * hm, axis=-1, keepdims=True) + EPS)


def _mlstm_scan(p_main, gates, conv_w, conv_b, c0, n0, m0, nseq, seq, state_out):
    ntok = nseq * seq
    nc = seq // M_CHUNK
    qblk = M_DQK
    st5 = lambda b, h: (b, 0, 0, h, 0, 0)
    in_specs = [
        pl.BlockSpec((seq, qblk), lambda b, h: (b, h)),
        pl.BlockSpec((seq, qblk), lambda b, h: (b, M_HEADS + h)),
        pl.BlockSpec((seq, M_DV), lambda b, h: (b, M_QK_DIM // M_DV + h)),
        pl.BlockSpec((1, nc, 4, M_CHUNK), lambda b, h: (h, b, 0, 0)),
        pl.BlockSpec((3, qblk), lambda b, h: (0, h)),
        pl.BlockSpec((3, qblk), lambda b, h: (0, M_HEADS + h)),
        pl.BlockSpec((1, qblk), lambda b, h: (0, h)),
        pl.BlockSpec((1, qblk), lambda b, h: (0, M_HEADS + h)),
        pl.BlockSpec((1, 1, 2, 1, M_DQK, M_DV), st5),
        pl.BlockSpec((1, 1, 2, 1, 1, M_DQK), st5),
        pl.BlockSpec((1, 1, 2, 1, 1, 1), st5),
    ]
    out_specs = [pl.BlockSpec((seq, M_DV), lambda b, h: (b, h))]
    out_shape = [jax.ShapeDtypeStruct((ntok, M_V_DIM), F32)]
    if state_out:
        out_specs += [
            pl.BlockSpec((1, 1, 2, 1, M_DQK, M_DV), st5),
            pl.BlockSpec((1, 1, 2, 1, 1, M_DQK), st5),
            pl.BlockSpec((1, 1, 2, 1, 1, 1), st5),
        ]
        out_shape += [
            jax.ShapeDtypeStruct((nseq, 1, 2, M_HEADS, M_DQK, M_DV), F32),
            jax.ShapeDtypeStruct((nseq, 1, 2, M_HEADS, 1, M_DQK), F32),
            jax.ShapeDtypeStruct((nseq, 1, 2, M_HEADS, 1, 1), F32),
        ]
    return pl.pallas_call(
        functools.partial(_scan_kernel, seq, state_out),
        grid=(nseq, M_HEADS),
        in_specs=in_specs,
        out_specs=out_specs,
        out_shape=out_shape,
        scratch_shapes=[
            pltpu.VMEM((seq, M_DQK), F32),
            pltpu.VMEM((seq, M_DQK), F32),
            pltpu.VMEM((seq, M_DV), F32),
            pltpu.VMEM((seq, M_DV), F32),
            pltpu.VMEM((M_DQK, M_DV), F32),
            pltpu.VMEM((M_DQK, M_DV), F32),
        ],
        compiler_params=_cparams("arbitrary", "arbitrary"),
        name="mlstm_scan_ctx" if state_out else "mlstm_scan_smp",
    )(p_main, p_main, p_main, gates, conv_w, conv_w, conv_b, conv_b, c0, n0, m0)


def _mout_kernel(x_ref, m_ref, hn_ref, o_ref, hg_ref, w_ref, y_ref):
    t = hn_ref[...] * hg_ref[...] * jax.nn.sigmoid(o_ref[...])
    y_ref[...] = x_ref[...] + m_ref[0, 2:3, :] * _dg(t.astype(BF16), w_ref[...])


def _mlstm_out(x, mods, cmap, hn, p_main, head_g, w_out):
    ntok = x.shape[0]
    tb = PROJ_TB
    return pl.pallas_call(
        _mout_kernel,
        grid=(ntok // tb,),
        in_specs=[
            pl.BlockSpec((tb, D_MODEL), lambda i: (i, 0)),
            pl.BlockSpec((1, 6, D_MODEL), cmap),
            pl.BlockSpec((tb, M_V_DIM), lambda i: (i, 0)),
            pl.BlockSpec((tb, M_V_DIM), lambda i: (i, (M_QK_DIM + M_V_DIM) // M_V_DIM)),
            pl.BlockSpec((1, M_V_DIM), lambda i: (0, 0)),
            pl.BlockSpec((M_V_DIM, D_MODEL), lambda i: (0, 0)),
        ],
        out_specs=pl.BlockSpec((tb, D_MODEL), lambda i: (i, 0)),
        out_shape=jax.ShapeDtypeStruct((ntok, D_MODEL), F32),
        compiler_params=_cparams("arbitrary"),
        name="mlstm_out",
    )(x, mods, hn, p_main, head_g, w_out)


def _aout_kernel(x_ref, m_ref, a_ref, w_ref, y_ref):
    y_ref[...] = x_ref[...] + m_ref[0, 2:3, :] * _dg(a_ref[...].astype(BF16), w_ref[...])


def _attn_out(x, mods, cmap, att, w_out):
    ntok = x.shape[0]
    tb = PROJ_TB
    return pl.pallas_call(
        _aout_kernel,
        grid=(ntok // tb,),
        in_specs=[
            pl.BlockSpec((tb, D_MODEL), lambda i: (i, 0)),
            pl.BlockSpec((1, 6, D_MODEL), cmap),
            pl.BlockSpec((tb, D_MODEL), lambda i: (i, 0)),
            pl.BlockSpec((D_MODEL, D_MODEL), lambda i: (0, 0)),
        ],
        out_specs=pl.BlockSpec((tb, D_MODEL), lambda i: (i, 0)),
        out_shape=jax.ShapeDtypeStruct((ntok, D_MODEL), F32),
        compiler_params=_cparams("arbitrary"),
        name="attn_out",
    )(x, mods, att, w_out)


def _aproj_kernel(rope, cache_out, x_ref, m_ref, g_ref, w_ref, gs_ref, qg_ref, kg_ref, *rest):
    rest = list(rest)
    if rope:
        cos_ref, sin_ref = rest[:2]
        rest = rest[2:]
    q_ref, k_ref, v_ref = rest[:3]
    rest = rest[3:]
    h = _modulate(x_ref[...], g_ref[...], m_ref[0, 0:1, :], m_ref[0, 1:2, :])
    p = _dg(h.astype(BF16), w_ref[...])
    nq = A_HEADS * A_HEAD_DIM
    nk = 2 * A_KV_HEADS * A_HEAD_DIM
    gs = gs_ref[...]
    lane = lax.broadcasted_iota(jnp.int32, (1, LANES), 1)
    first_half = (lane % A_ROPE_AXIS) < (A_ROPE_AXIS // 2)

    def norm_rope(xs, gain):
        sq = xs * xs
        s1 = sq.astype(BF16)
        r1 = sq - s1.astype(F32)
        s2 = r1.astype(BF16)
        s3 = (r1 - s2.astype(F32)).astype(BF16)
        ss = _dg(s1, gs) + (_dg(s2, gs) + _dg(s3, gs))
        y = xs * lax.rsqrt(ss * (1.0 / A_HEAD_DIM) + EPS) * gain
        if rope:
            half = A_ROPE_AXIS // 2
            swapped = jnp.where(first_half, pltpu.roll(y, LANES - half, 1), pltpu.roll(y, half, 1))
            y = y * cos_ref[...] + swapped * sin_ref[...]
        return y

    for s in range(nq // LANES):
        q_ref[:, s * LANES:(s + 1) * LANES] = norm_rope(p[:, s * LANES:(s + 1) * LANES], qg_ref[...])
    for s in range(nk // LANES):
        kn = norm_rope(p[:, nq + s * LANES:nq + (s + 1) * LANES], kg_ref[...])
        k_ref[:, s * LANES:(s + 1) * LANES] = kn
        vs = p[:, nq + nk + s * LANES:nq + nk + (s + 1) * LANES]
        v_ref[:, s * LANES:(s + 1) * LANES] = vs
        if cache_out:
            kc_ref, vc_ref = rest
            kc_ref[0, 0, s] = kn[:, :A_HEAD_DIM]
            vc_ref[0, 0, s] = vs[:, :A_HEAD_DIM]


def _attn_project(x, mods, cmap, g, w_qkv2, gsum, qg, kg, rope_tabs, nseq, seq, cache_out):
    ntok = x.shape[0]
    tb = PROJ_TB
    nq = A_HEADS * A_HEAD_DIM
    nk = 2 * A_KV_HEADS * A_HEAD_DIM
    per = seq // tb
    rope = rope_tabs is not None
    in_specs = [
        pl.BlockSpec((tb, D_MODEL), lambda i: (i, 0)),
        pl.BlockSpec((1, 6, D_MODEL), cmap),
        pl.BlockSpec((1, D_MODEL), lambda i: (0, 0)),
        pl.BlockSpec((D_MODEL, nq + 2 * nk), lambda i: (0, 0)),
        pl.BlockSpec((LANES, LANES), lambda i: (0, 0)),
        pl.BlockSpec((1, LANES), lambda i: (0, 0)),
        pl.BlockSpec((1, LANES), lambda i: (0, 0)),
    ]
    args = [x, mods, g, w_qkv2, gsum, qg, kg]
    if rope:
        in_specs += [pl.BlockSpec((tb, LANES), lambda i: (i % per, 0))] * 2
        args += list(rope_tabs)
    out_specs = [
        pl.BlockSpec((tb, nq), lambda i: (i, 0)),
        pl.BlockSpec((tb, nk), lambda i: (i, 0)),
        pl.BlockSpec((tb, nk), lambda i: (i, 0)),
    ]
    out_shape = [
        jax.ShapeDtypeStruct((ntok, nq), F32),
        jax.ShapeDtypeStruct((ntok, nk), F32),
        jax.ShapeDtypeStruct((ntok, nk), F32),
    ]
    if cache_out:
        assert seq == tb
        cspec = pl.BlockSpec((1, 1, A_KV_HEADS, seq, A_HEAD_DIM), lambda i: (i, 0, 0, 0, 0))
        cshape = jax.ShapeDtypeStruct((nseq, 1, A_KV_HEADS, seq, A_HEAD_DIM), F32)
        out_specs += [cspec, cspec]
        out_shape += [cshape, cshape]
    return pl.pallas_call(
        functools.partial(_aproj_kernel, rope, cache_out),
        grid=(ntok // tb,),
        in_specs=in_specs,
        out_specs=out_specs,
        out_shape=out_shape,
        compiler_params=_cparams("arbitrary"),
        name="attn_project_ctx" if cache_out else "attn_project_smp",
    )(*args)


def _attn_kernel(cached, q_ref, k_ref, v_ref, *rest):
    if cached:
        kc_ref, vc_ref, o_ref = rest
    else:
        (o_ref,) = rest
    lane = lax.broadcasted_iota(jnp.int32, (1, LANES), 1)
    q = q_ref[...] * (A_HEAD_DIM ** -0.5)
    k = k_ref[...].astype(BF16)
    v = v_ref[...]
    if cached:
        kc = kc_ref[0, 0].astype(BF16)
        vc = vc_ref[0, 0]
    acc = jnp.zeros(q.shape, F32)
    for e in range(2):
        sel = (lane < A_HEAD_DIM) if e == 0 else (lane >= A_HEAD_DIM)
        qe = jnp.where(sel, q, 0.0).astype(BF16)
        s = _dg(qe, k, 1, 1)
        mx = jnp.max(s, axis=1, keepdims=True)
        if cached:
            sc = _dg(qe, kc, 1, 1)
            mx = jnp.maximum(mx, jnp.max(sc, axis=1, keepdims=True))
        p = jnp.exp(s - mx)
        l = jnp.sum(p, axis=1, keepdims=True)
        o = _dg(p.astype(BF16), jnp.where(sel, v, 0.0).astype(BF16))
        if cached:
            pc = jnp.exp(sc - mx)
            l = l + jnp.sum(pc, axis=1, keepdims=True)
            o = o + _dg(pc.astype(BF16), jnp.where(sel, vc, 0.0).astype(BF16))
        acc = acc + o / l
    o_ref[...] = acc


def _attention(q, k2, v2, cache, nseq, seq):
    ntok = q.shape[0]
    qb = min(ATTN_QB, seq)
    nqb = seq // qb
    npair = A_HEADS // 2
    in_specs = [
        pl.BlockSpec((qb, LANES), lambda b, hp, j: (b * nqb + j, hp)),
        pl.BlockSpec((seq, LANES), lambda b, hp, j: (b, hp // 2)),
        pl.BlockSpec((seq, LANES), lambda b, hp, j: (b, hp // 2)),
    ]
    args = [q, k2, v2]
    if cache is not None:
        past = cache[0].shape[2]
        cspec = pl.BlockSpec((1, 1, past, LANES), lambda b, hp, j: (b, hp // 2, 0, 0))
        in_specs += [cspec, cspec]
        args += list(cache)
    return pl.pallas_call(
        functools.partial(_attn_kernel, cache is not None),
        grid=(nseq, npair, nqb),
        in_specs=in_specs,
        out_specs=pl.BlockSpec((qb, LANES), lambda b, hp, j: (b * nqb + j, hp)),
        out_shape=jax.ShapeDtypeStruct((ntok, A_HEADS * A_HEAD_DIM), F32),
        compiler_params=_cparams("arbitrary", "arbitrary", "arbitrary"),
        name="attention_smp" if cache is not None else "attention_ctx",
    )(*args)


def _batcher_pairs(n):
    pairs = []
    p = 1
    while p < n:
        k = p
        while k >= 1:
            for j in range(k % p, n - k, 2 * k):
                for i in range(min(k, n - j - k)):
                    if (i + j) // (2 * p) == (i + j + k) // (2 * p):
                        pairs.append((i + j, i + j + k))
            k //= 2
        p *= 2
    return pairs


def _sort_levels(levels):
    lv = list(levels)
    for i, j in _batcher_pairs(len(lv)):
        lv[i], lv[j] = jnp.maximum(lv[i], lv[j]), jnp.minimum(lv[i], lv[j])
    return lv


def _pop_top(levels, k, sub):
    lv = list(levels)
    n = len(lv)
    outs = []
    for it in range(k):
        head = lv[0]
        m = jnp.max(head, axis=0, keepdims=True)
        outs.append(m)
        rem = k - 1 - it
        if rem == 0:
            break
        first = jnp.min(jnp.where(head == m, sub, 8.0), axis=0, keepdims=True)
        pop = sub == first
        for q in range(min(n, rem)):
            nxt = lv[q + 1] if q + 1 < n else -jnp.inf
            lv[q] = jnp.where(pop, nxt, lv[q])
    return outs


def _cand_levels(a, b, sub):
    k1 = P_TOPK + 1
    acol = jnp.where(sub == 0.0, a[0], jnp.where(sub == 1.0, a[1], jnp.where(sub == 2.0, a[2], a[3])))
    bcol = jnp.where(sub == 4.0, b[0], jnp.where(sub == 5.0, b[1], b[2]))
    low = sub < 4.0
    levels = []
    for lvl in range(k1):
        na = min(4, k1 // (lvl + 1))
        nb = min(3, k1 // (lvl + 5)) if lvl + 4 < k1 else 0
        valid = sub < float(na)
        x = acol + b[lvl]
        if nb:
            x = jnp.where(low, x, a[lvl + 4] + bcol)
            valid = valid | ((sub >= 4.0) & (sub < float(4 + nb)))
        levels.append(jnp.where(valid, x, -jnp.inf))
    return levels


def _pproj_kernel(x_ref, m_ref, g_ref, wqh_ref, wql_ref, skh_ref, skl_ref,
                  ht_ref, thr_ref, e1_ref, s2_ref):
    h = _modulate(x_ref[...], g_ref[...], m_ref[0, 3:4, :], m_ref[0, 4:5, :])
    ht_ref[...] = h.T.astype(BF16)
    q = _mm3w(h, wqh_ref[...], wql_ref[...])
    k1 = P_TOPK + 1
    sub = lax.broadcasted_iota(jnp.int32, (8, LANES), 0).astype(F32)
    for p in range(P_HEADS):
        sc = []
        for hf in range(2):
            ph = 2 * p + hf
            qh, ql = _split2(q[:, ph * LANES:(ph + 1) * LANES])
            kh = skh_ref[ph]
            sc.append(_dg(kh, qh, 1, 1) + (_dg(kh, ql, 1, 1) + _dg(skl_ref[ph], qh, 1, 1)))
        for lt in range(q.shape[0] // LANES):
            ls = slice(lt * LANES, (lt + 1) * LANES)
            s1 = sc[0][:, ls]
            s2 = sc[1][:, ls]
            a = _pop_top(_sort_levels([s1[8 * r:8 * r + 8] for r in range(P_NKEYS // 8)]), k1, sub)
            b = _pop_top(_sort_levels([s2[8 * r:8 * r + 8] for r in range(P_NKEYS // 8)]), k1, sub)
            v = _pop_top(_cand_levels(a, b, sub), k1, sub)
            tau = 0.5 * (v[P_TOPK - 1] + v[P_TOPK])
            z = jnp.ones_like(tau)
            for kk in range(1, P_TOPK):
                z = z + jnp.exp(v[kk] - v[0])
            thr_ref[p, :, ls] = ((tau - b[0]) - s1) * LOG2E
            e1_ref[p, :, ls] = jnp.exp(s1 - a[0]) * (0.5 / z)
            s2_ref[p, :, ls] = (s2 - b[0]) * LOG2E


def _peer_project(x, mods, cmap, g, wq_hi, wq_lo, sk_hi, sk_lo):
    ntok = x.shape[0]
    tb = PROJ_TB
    nq = P_HEADS * P_DKEY
    sel_spec = pl.BlockSpec((P_HEADS, P_NKEYS, tb), lambda i: (0, 0, i))
    sel_shape = jax.ShapeDtypeStruct((P_HEADS, P_NKEYS, ntok), F32)
    return pl.pallas_call(
        _pproj_kernel,
        grid=(ntok // tb,),
        in_specs=[
            pl.BlockSpec((tb, D_MODEL), lambda i: (i, 0)),
            pl.BlockSpec((1, 6, D_MODEL), cmap),
            pl.BlockSpec((1, D_MODEL), lambda i: (0, 0)),
            pl.BlockSpec((D_MODEL, nq), lambda i: (0, 0)),
            pl.BlockSpec((D_MODEL, nq), lambda i: (0, 0)),
            pl.BlockSpec((2 * P_HEADS, P_NKEYS, P_DKEY // 2), lambda i: (0, 0, 0)),
            pl.BlockSpec((2 * P_HEADS, P_NKEYS, P_DKEY // 2), lambda i: (0, 0, 0)),
        ],
        out_specs=[pl.BlockSpec((D_MODEL, tb), lambda i: (0, i)), sel_spec, sel_spec, sel_spec],
        out_shape=[jax.ShapeDtypeStruct((D_MODEL, ntok), BF16), sel_shape, sel_shape, sel_shape],
        compiler_params=_cparams("arbitrary"),
        name="peer_project",
    )(x, mods, g, wq_hi, wq_lo, sk_hi, sk_lo)


def _pdense_kernel(final, ne, x_ref, m_ref, ht_ref, thr_ref, e1_ref, s2_ref, u_ref, vt_ref, fg_ref,
                   y_ref, act0_s, act1_s, gt0_s, gt1_s, acc_s):
    s = pl.program_id(0)
    nchunk = pl.num_programs(0) - 2
    jc = jnp.clip(s - 2, 0, nchunk - 1) % ne

    @pl.when(s == 0)
    def _():
        act1_s[...] = jnp.zeros(act1_s.shape, F32)
        gt1_s[...] = jnp.zeros(gt1_s.shape, BF16)
        acc_s[...] = jnp.zeros(acc_s.shape, F32)

    @pl.when(s % 2 == 0)
    def _():
        _pdense_step(u_ref, ht_ref, thr_ref, e1_ref, s2_ref, vt_ref, act0_s, act1_s, gt0_s, gt1_s, acc_s)

    @pl.when(s % 2 == 1)
    def _():
        _pdense_step(u_ref, ht_ref, thr_ref, e1_ref, s2_ref, vt_ref, act1_s, act0_s, gt1_s, gt0_s, acc_s)

    @pl.when((s >= 2) & (jc == ne - 1))
    def _():
        xn = x_ref[...] + m_ref[0, 5:6, :] * acc_s[...].T
        if final:
            xn = xn * lax.rsqrt(jnp.mean(xn * xn, axis=-1, keepdims=True) + EPS) * fg_ref[...]
        y_ref[...] = xn
        acc_s[...] = jnp.zeros(acc_s.shape, F32)


def _pdense_step(u_ref, ht_ref, thr_ref, e1_ref, s2_ref, vt_ref, act_w, act_r, gt_w, gt_r, acc_s):
    ec, td = act_w.shape
    na, nb = 4, 32
    half = td // 2

    def half_body(hh, carry):
        h0 = hh * half
        hs = pl.ds(pl.multiple_of(h0, half), half)
        act_w[:, hs] = _dg(u_ref[...], ht_ref[:, hs])
        acc_s[:, hs] += _dg(vt_ref[...], gt_r[:, hs])
        for lt in range(half // LANES):
            ls = pl.ds(pl.multiple_of(h0 + lt * LANES, LANES), LANES)
            for aq in range(ec // P_NKEYS // na):
                for bq in range(P_NKEYS // nb):
                    bs = slice(bq * nb, (bq + 1) * nb)
                    w = [jnp.zeros((nb, LANES), F32) for _ in range(na)]
                    for p in range(P_HEADS):
                        s2 = s2_ref[p, bs, ls]
                        e2 = jnp.exp2(s2)
                        for ai in range(na):
                            al = aq * na + ai
                            sel = jnp.where(s2 >= thr_ref[p, al:al + 1, ls], e2, 0.0)
                            w[ai] = w[ai] + sel * e1_ref[p, al:al + 1, ls]
                    for ai in range(na):
                        r0 = (aq * na + ai) * P_NKEYS + bq * nb
                        xa = act_r[r0:r0 + nb, ls]
                        gel2 = xa * (1.0 + lax.erf(xa * (2.0 ** -0.5)))
                        gt_w[r0:r0 + nb, ls] = (w[ai] * gel2).astype(BF16)
        return carry

    lax.fori_loop(0, td // half, half_body, 0)


def _peer_dense(x, mods, cmap, ht, thr, e1, s2, u, vt, layer, final_g, final):
    ntok = x.shape[0]
    td = PEER_TD
    ec = PEER_EC
    ne = u.shape[1] // ec
    nchunk = (ntok // td) * ne
    ca = lambda s: jnp.minimum(s, nchunk - 1)
    cb = lambda s: jnp.clip(s - 1, 0, nchunk - 1)
    cc = lambda s: jnp.clip(s - 2, 0, nchunk - 1)
    sel_spec = pl.BlockSpec((P_HEADS, P_NKEYS, td), lambda s: (0, 0, cb(s) // ne))
    row_spec = pl.BlockSpec((P_HEADS, ec // P_NKEYS, td), lambda s: (0, cb(s) % ne, cb(s) // ne))
    return pl.pallas_call(
        functools.partial(_pdense_kernel, final, ne),
        grid=(nchunk + 2,),
        in_specs=[
            pl.BlockSpec((td, D_MODEL), lambda s: (cc(s) // ne, 0)),
            pl.BlockSpec((1, 6, D_MODEL), lambda s: cmap(cc(s) // ne)),
            pl.BlockSpec((D_MODEL, td), lambda s: (0, ca(s) // ne)),
            row_spec, row_spec, sel_spec,
            pl.BlockSpec((None, ec, D_MODEL), lambda s: (layer, ca(s) % ne, 0)),
            pl.BlockSpec((None, D_MODEL, ec), lambda s: (layer, 0, cc(s) % ne)),
            pl.BlockSpec((1, D_MODEL), lambda s: (0, 0)),
        ],
        out_specs=pl.BlockSpec((td, D_MODEL), lambda s: (cc(s) // ne, 0)),
        out_shape=jax.ShapeDtypeStruct((ntok, D_MODEL), F32),
        scratch_shapes=[
            pltpu.VMEM((ec, td), F32),
            pltpu.VMEM((ec, td), F32),
            pltpu.VMEM((ec, td), BF16),
            pltpu.VMEM((ec, td), BF16),
            pltpu.VMEM((D_MODEL, td), F32),
        ],
        compiler_params=_cparams("arbitrary"),
        name="peer_dense",
    )(x, mods, ht, thr, e1, s2, u, vt, final_g)


def _rope_tables(seq):
    n_rows = seq // GRID_W
    rows = jnp.repeat(jnp.arange(n_rows, dtype=F32), GRID_W)
    cols = jnp.tile(jnp.arange(GRID_W, dtype=F32), n_rows)
    half = A_ROPE_AXIS // 2
    inv_freq = ROPE_THETA ** (-jnp.arange(half, dtype=F32) / half)
    ar = rows[:, None] * inv_freq
    ac = cols[:, None] * inv_freq
    cos = jnp.concatenate([jnp.cos(ar), jnp.cos(ar), jnp.cos(ac), jnp.cos(ac)], axis=-1)
    sin = jnp.concatenate([-jnp.sin(ar), jnp.sin(ar), -jnp.sin(ac), jnp.sin(ac)], axis=-1)
    return jnp.tile(cos, (1, LANES // A_HEAD_DIM)), jnp.tile(sin, (1, LANES // A_HEAD_DIM))


def _hi_lo(w):
    hi = w.astype(BF16)
    return hi, (w - hi.astype(F32)).astype(BF16)


def kernel(x_prompt, x_sample, state_mlstm_C, state_mlstm_n, state_mlstm_m, cache_attn_k, cache_attn_v, c, c_ctx, ada_w, ada_b, norm_g, final_g, mlstm_w_in, mlstm_conv_w, mlstm_conv_b, mlstm_gate_b, mlstm_head_g, mlstm_w_out, attn_w_qkv, attn_q_g, attn_k_g, attn_w_out, peer_w_q, peer_subkeys, peer_u, peer_v):
    nb, seq_c, d = x_prompt.shape
    ndb, seq_s, _ = x_sample.shape
    assert d == D_MODEL and seq_c % PROJ_TB == 0 and seq_s % PROJ_TB == 0
    assert (nb * seq_c) % PEER_TD == 0 and seq_s % PEER_TD == 0

    nrow = -(-(1 + ndb) // 8) * 8
    cond = jnp.concatenate([c_ctx[None, :], c, jnp.zeros((nrow - 1 - ndb, d), F32)], axis=0)
    mods_all = _ada_mods(cond, ada_w, ada_b)

    groups = [
        dict(x=x_prompt.reshape(nb * seq_c, d), nseq=nb, seq=seq_c, ctx=True),
        dict(x=x_sample.reshape(ndb * seq_s, d), nseq=ndb, seq=seq_s, ctx=False),
    ]

    def cmap(gr, tb):
        return _cond_map(0, None) if gr["ctx"] else _cond_map(1, gr["seq"] // tb)

    outs = {}
    fg = final_g.reshape(1, d)

    w_in = mlstm_w_in[0]
    w_main = w_in[:, :M_MAIN_DIM].astype(BF16)
    perm = jnp.arange(4 * M_HEADS).reshape(4, M_HEADS).T.reshape(-1)
    wg_t = w_in[:, M_MAIN_DIM:].T[perm]
    wg_hi, wg_lo = _hi_lo(wg_t)
    gate_b = mlstm_gate_b[0][perm].reshape(-1, 1)
    w_mout = mlstm_w_out[0].astype(BF16)
    for gr in groups:
        nseq, seq = gr["nseq"], gr["seq"]
        mods = mods_all[0]
        p_main, gt = _mlstm_project(gr["x"], mods, cmap(gr, PROJ_TB), norm_g[0, 0].reshape(1, d),
                                    w_main, wg_hi, wg_lo, gate_b)
        nc = seq // M_CHUNK
        gates = gt.reshape(M_HEADS, 4, nseq * nc, M_CHUNK).transpose(0, 2, 1, 3)
        if gr["ctx"]:
            c0 = jnp.zeros((nseq, 1, 2, M_HEADS, M_DQK, M_DV), F32)
            n0 = jnp.zeros((nseq, 1, 2, M_HEADS, 1, M_DQK), F32)
            m0 = jnp.zeros((nseq, 1, 2, M_HEADS, 1, 1), F32)
        else:
            c0 = state_mlstm_C.astype(F32)
            n0 = state_mlstm_n.astype(F32).reshape(nseq, -1, 2, M_HEADS, 1, M_DQK)
            m0 = state_mlstm_m.astype(F32).reshape(nseq, -1, 2, M_HEADS, 1, 1)
        res = _mlstm_scan(p_main, gates, mlstm_conv_w[0], mlstm_conv_b[0].reshape(1, -1),
                          c0, n0, m0, nseq, seq, gr["ctx"])
        if gr["ctx"]:
            hn, c_new, n_new, m_new = res
            outs["C"] = c_new
            outs["n"] = n_new.reshape(nseq, 1, 2, M_HEADS, M_DQK)
            outs["m"] = m_new.reshape(nseq, 1, 2, M_HEADS)
        else:
            (hn,) = res
        gr["x"] = _mlstm_out(gr["x"], mods, cmap(gr, PROJ_TB), hn, p_main,
                             mlstm_head_g[0].reshape(1, -1), w_mout)

    def peer(layer, final):
        wq_hi, wq_lo = _hi_lo(peer_w_q[layer])
        sk_hi, sk_lo = _hi_lo(peer_subkeys[layer].reshape(2 * P_HEADS, P_NKEYS, P_DKEY // 2))
        for gr in groups:
            mods = mods_all[layer]
            ht, thr, e1, s2 = _peer_project(gr["x"], mods, cmap(gr, PROJ_TB), norm_g[layer, 1].reshape(1, d),
                                            wq_hi, wq_lo, sk_hi, sk_lo)
            gr["x"] = _peer_dense(gr["x"], mods, cmap(gr, PEER_TD), ht, thr, e1, s2, u_all, vt_all, layer, fg, final)

    u_all = peer_u.astype(BF16)
    vt_all = jnp.swapaxes(peer_v.astype(BF16), 1, 2)
    peer(0, False)

    wqkv = attn_w_qkv[0]
    nq = A_HEADS * A_HEAD_DIM
    nkv = A_KV_HEADS * A_HEAD_DIM
    dup = lambda w: jnp.tile(w.reshape(d, A_KV_HEADS, 1, A_HEAD_DIM), (1, 1, 2, 1)).reshape(d, 2 * nkv)
    w_qkv2 = jnp.concatenate([wqkv[:, :nq], dup(wqkv[:, nq:nq + nkv]), dup(wqkv[:, nq + nkv:])], axis=1).astype(BF16)
    li = jnp.arange(LANES)
    gsum = (li[:, None] // A_HEAD_DIM == li[None, :] // A_HEAD_DIM).astype(BF16)
    qg = jnp.tile(attn_q_g[0], LANES // A_HEAD_DIM).reshape(1, LANES)
    kg = jnp.tile(attn_k_g[0], LANES // A_HEAD_DIM).reshape(1, LANES)
    w_aout = attn_w_out[0].astype(BF16)
    for gr in groups:
        nseq, seq = gr["nseq"], gr["seq"]
        mods = mods_all[1]
        tabs = None if gr["ctx"] else _rope_tables(seq)
        res = _attn_project(gr["x"], mods, cmap(gr, PROJ_TB), norm_g[1, 0].reshape(1, d), w_qkv2, gsum, qg, kg,
                            tabs, nseq, seq, gr["ctx"])
        if gr["ctx"]:
            q, k2, v2, kc_new, vc_new = res
            outs["k"], outs["v"] = kc_new, vc_new
            cache = None
        else:
            q, k2, v2 = res
            cache = (jnp.tile(cache_attn_k[:, 0].astype(F32), (1, 1, 1, 2)),
                     jnp.tile(cache_attn_v[:, 0].astype(F32), (1, 1, 1, 2)))
        att = _attention(q, k2, v2, cache, nseq, seq)
        gr["x"] = _attn_out(gr["x"], mods, cmap(gr, PROJ_TB), att, w_aout)

    peer(1, True)

    y_prompt = groups[0]["x"].reshape(nb, seq_c, d)
    y_sample = groups[1]["x"].reshape(ndb, seq_s, d)
    return (y_prompt, y_sample, outs["C"], outs["n"], outs["m"], outs["k"], outs["v"])
```

```python
import functools

import jax
import jax.numpy as jnp
from jax import lax
from jax.experimental import pallas as pl
from jax.experimental.pallas import tpu as pltpu

F32 = jnp.float32
BF16 = jnp.bfloat16

EPS = 1e-6
LOG2E = 1.4426950408889634
D_MODEL = 1024
GRID_W = 64
ROPE_THETA = 10000.0

M_HEADS = 4
M_DQK = 128
M_DV = 256
M_CHUNK = 128
M_QK_DIM = 2 * M_HEADS * M_DQK
M_V_DIM = M_HEADS * M_DV
M_MAIN_DIM = M_QK_DIM + 2 * M_V_DIM

A_HEADS = 16
A_KV_HEADS = 4
A_HEAD_DIM = 64
A_ROPE_AXIS = A_HEAD_DIM // 2

P_HEADS = 8
P_NKEYS = 128
P_DKEY = 256
P_TOPK = 16

LANES = 128
PROJ_TB = 256
PEER_TD = 512
PEER_EC = 2048
ATTN_QB = 512
VMEM_LIMIT = 56 * 1024 * 1024


def _cparams(*sem):
    return pltpu.CompilerParams(dimension_semantics=sem, vmem_limit_bytes=VMEM_LIMIT)


def _dg(a, b, ca=1, cb=0):
    return lax.dot_general(a, b, (((ca,), (cb,)), ((), ())), preferred_element_type=F32)


def _split2(x):
    hi = x.astype(BF16)
    lo = (x - hi.astype(F32)).astype(BF16)
    return hi, lo


def _mm3(a, b, ca=1, cb=0):
    ah, al = _split2(a)
    bh, bl = _split2(b)
    return _dg(ah, bh, ca, cb) + (_dg(ah, bl, ca, cb) + _dg(al, bh, ca, cb))


def _mm3w(a, wh, wl, ca=1, cb=0):
    ah, al = _split2(a)
    return _dg(ah, wh, ca, cb) + (_dg(ah, wl, ca, cb) + _dg(al, wh, ca, cb))


def _modulate(x, g, shift, scale):
    y = x * lax.rsqrt(jnp.mean(x * x, axis=-1, keepdims=True) + EPS)
    return (y * g) * (1.0 + scale) + shift


def _cond_map(base, per):
    if per is None:
        return lambda i, *_: (base, 0, 0)
    return lambda i, *_: (base + i // per, 0, 0)


def _ada_kernel(c_ref, w_ref, b_ref, o_ref):
    c = c_ref[...]
    a = c * jax.nn.sigmoid(c)
    o_ref[0] = _mm3(a, w_ref[0]) + b_ref[0]


def _ada_mods(cond, ada_w, ada_b):
    depth, d, n = ada_w.shape
    tn = 512
    rows = cond.shape[0]
    out = pl.pallas_call(
        _ada_kernel,
        grid=(depth, n // tn),
        in_specs=[
            pl.BlockSpec((rows, d), lambda l, j: (0, 0)),
            pl.BlockSpec((1, d, tn), lambda l, j: (l, 0, j)),
            pl.BlockSpec((1, 1, tn), lambda l, j: (l, 0, j)),
        ],
        out_specs=pl.BlockSpec((1, rows, tn), lambda l, j: (l, 0, j)),
        out_shape=jax.ShapeDtypeStruct((depth, rows, n), F32),
        compiler_params=_cparams("arbitrary", "arbitrary"),
        name="ada_mods",
    )(cond, ada_w, ada_b.reshape(depth, 1, n))
    return out.reshape(depth, rows, 6, d)


def _mproj_kernel(x_ref, m_ref, g_ref, w_ref, wgh_ref, wgl_ref, gb_ref, p_ref, gt_ref):
    h = _modulate(x_ref[...], g_ref[...], m_ref[0, 0:1, :], m_ref[0, 1:2, :])
    p_ref[...] = _dg(h.astype(BF16), w_ref[...])
    hh, hl = _split2(h)
    wh = wgh_ref[...]
    gt = _dg(wh, hh, 1, 1) + (_dg(wh, hl, 1, 1) + _dg(wgl_ref[...], hh, 1, 1))
    gt_ref[...] = gt + gb_ref[...]


def _mlstm_project(x, mods, cmap, g, w_main, wg_hi, wg_lo, gate_b):
    ntok = x.shape[0]
    tb = PROJ_TB
    ng = wg_hi.shape[0]
    return pl.pallas_call(
        _mproj_kernel,
        grid=(ntok // tb,),
        in_specs=[
            pl.BlockSpec((tb, D_MODEL), lambda i: (i, 0)),
            pl.BlockSpec((1, 6, D_MODEL), cmap),
            pl.BlockSpec((1, D_MODEL), lambda i: (0, 0)),
            pl.BlockSpec((D_MODEL, M_MAIN_DIM), lambda i: (0, 0)),
            pl.BlockSpec((ng, D_MODEL), lambda i: (0, 0)),
            pl.BlockSpec((ng, D_MODEL), lambda i: (0, 0)),
            pl.BlockSpec((ng, 1), lambda i: (0, 0)),
        ],
        out_specs=[
            pl.BlockSpec((tb, M_MAIN_DIM), lambda i: (i, 0)),
            pl.BlockSpec((ng, tb), lambda i: (0, i)),
        ],
        out_shape=[
            jax.ShapeDtypeStruct((ntok, M_MAIN_DIM), F32),
            jax.ShapeDtypeStruct((ng, ntok), F32),
        ],
        compiler_params=_cparams("arbitrary"),
        name="mlstm_project",
    )(x, mods, g, w_main, wg_hi, wg_lo, gate_b)


def _scan_kernel(seq, state_out, qp_ref, kp_ref, v_ref, g_ref, cwq_ref, cwk_ref, cbq_ref, cbk_ref,
                 c0_ref, n0_ref, m0_ref, hn_ref, *rest):
    if state_out:
        co_ref, no_ref, mo_ref, q_s, k_s, hf_s, hb_s, cf_s, cb_s = rest
    else:
        q_s, k_s, hf_s, hb_s, cf_s, cb_s = rest
    h_dir = (hf_s, hb_s)
    c_dir = (cf_s, cb_s)
    nc = seq // M_CHUNK
    rows = lax.broadcasted_iota(jnp.int32, (seq, 1), 0)

    def conv(x, w_ref, b_ref):
        xm1 = jnp.where(rows == 0, 0.0, pltpu.roll(x, 1, 0))
        xp1 = jnp.where(rows == seq - 1, 0.0, pltpu.roll(x, seq - 1, 0))
        y = xm1 * w_ref[0:1, :] + x * w_ref[1:2, :] + xp1 * w_ref[2:3, :] + b_ref[...]
        return y * jax.nn.sigmoid(y)

    q_s[...] = conv(qp_ref[...], cwq_ref, cbq_ref)
    k_s[...] = conv(kp_ref[...], cwk_ref, cbk_ref) * (M_DQK ** -0.5)

    ti = lax.broadcasted_iota(jnp.int32, (M_CHUNK, M_CHUNK), 0)
    si = lax.broadcasted_iota(jnp.int32, (M_CHUNK, M_CHUNK), 1)
    eye = ti == si

    def row2col(r):
        return jnp.sum(jnp.where(eye, r, 0.0), axis=1, keepdims=True)

    def col2row(c):
        return jnp.sum(jnp.where(eye, c, 0.0), axis=0, keepdims=True)

    def chunk_step(dirn, c, n, m):
        mask = (si <= ti) if dirn == 0 else (si >= ti)
        c_s = c_dir[dirn]
        r0 = pl.multiple_of(c * M_CHUNK, M_CHUNK)
        g4 = g_ref[0, c]
        ig = g4[2 * dirn:2 * dirn + 1, :]
        fp = g4[2 * dirn + 1:2 * dirn + 2, :]
        lf = jnp.minimum(fp, 0.0) - jnp.log1p(jnp.exp(-jnp.abs(fp)))
        b_col = jnp.sum(jnp.where(mask, lf, 0.0), axis=1, keepdims=True)
        b_row = col2row(b_col)
        b_end = jnp.sum(lf, axis=1, keepdims=True)
        qc = q_s[pl.ds(r0, M_CHUNK), :]
        kc = k_s[pl.ds(r0, M_CHUNK), :]
        vc = v_ref[pl.ds(r0, M_CHUNK), :]
        d = jnp.where(mask, b_col - b_row + ig, -jnp.inf)
        inter = b_col + m
        m_t = jnp.maximum(inter, jnp.max(d, axis=1, keepdims=True))
        qb = qc.astype(BF16)
        w = jnp.exp(d - m_t) * _dg(qb, kc.astype(BF16), 1, 1)
        a = jnp.exp(inter - m_t)
        c_old = c_s[...]
        num = a * _dg(qb, c_old.astype(BF16)) + _dg(w.astype(BF16), vc.astype(BF16))
        den = a * jnp.sum(qc * n, axis=1, keepdims=True) + jnp.sum(w, axis=1, keepdims=True)
        h_dir[dirn][pl.ds(r0, M_CHUNK), :] = num / jnp.maximum(jnp.abs(den), jnp.exp(-m_t))
        to_end = b_end - b_row + ig
        m_new = jnp.maximum(b_end + m, jnp.max(to_end, axis=1, keepdims=True))
        decay = jnp.exp(b_end + m - m_new)
        kw = kc * row2col(jnp.exp(to_end - m_new))
        c_s[...] = decay * c_old + _mm3(kw.T, vc)
        n_new = decay * n + jnp.sum(kw, axis=0, keepdims=True)
        return n_new, m_new

    for dirn in range(2):
        c_dir[dirn][...] = c0_ref[0, 0, dirn, 0]

    def body(it, carry):
        nf, mf, nb, mb = carry
        nf, mf = chunk_step(0, it, nf, mf)
        nb, mb = chunk_step(1, nc - 1 - it, nb, mb)
        return nf, mf, nb, mb

    init = (n0_ref[0, 0, 0, 0], m0_ref[0, 0, 0, 0], n0_ref[0, 0, 1, 0], m0_ref[0, 0, 1, 0])
    fin = lax.fori_loop(0, nc, body, init)
    if state_out:
        for dirn in range(2):
            co_ref[0, 0, dirn, 0] = c_dir[dirn][...]
            no_ref[0, 0, dirn, 0] = fin[2 * dirn]
            mo_ref[0, 0, dirn, 0] = fin[2 * dirn + 1]

    hm = hf_s[...] + hb_s[...]
    hn_ref[...] = hm * lax.rsqrt(jnp.mean(hm * hm, axis=-1, keepdims=True) + EPS)


def _mlstm_scan(p_main, gates, conv_w, conv_b, c0, n0, m0, nseq, seq, state_out):
    ntok = nseq * seq
    nc = seq // M_CHUNK
    qblk = M_DQK
    st5 = lambda b, h: (b, 0, 0, h, 0, 0)
    in_specs = [
        pl.BlockSpec((seq, qblk), lambda b, h: (b, h)),
        pl.BlockSpec((seq, qblk), lambda b, h: (b, M_HEADS + h)),
        pl.BlockSpec((seq, M_DV), lambda b, h: (b, M_QK_DIM // M_DV + h)),
        pl.BlockSpec((1, nc, 4, M_CHUNK), lambda b, h: (h, b, 0, 0)),
        pl.BlockSpec((3, qblk), lambda b, h: (0, h)),
        pl.BlockSpec((3, qblk), lambda b, h: (0, M_HEADS + h)),
        pl.BlockSpec((1, qblk), lambda b, h: (0, h)),
        pl.BlockSpec((1, qblk), lambda b, h: (0, M_HEADS + h)),
        pl.BlockSpec((1, 1, 2, 1, M_DQK, M_DV), st5),
        pl.BlockSpec((1, 1, 2, 1, 1, M_DQK), st5),
        pl.BlockSpec((1, 1, 2, 1, 1, 1), st5),
    ]
    out_specs = [pl.BlockSpec((seq, M_DV), lambda b, h: (b, h))]
    out_shape = [jax.ShapeDtypeStruct((ntok, M_V_DIM), F32)]
    if state_out:
        out_specs += [
            pl.BlockSpec((1, 1, 2, 1, M_DQK, M_DV), st5),
            pl.BlockSpec((1, 1, 2, 1, 1, M_DQK), st5),
            pl.BlockSpec((1, 1, 2, 1, 1, 1), st5),
        ]
        out_shape += [
            jax.ShapeDtypeStruct((nseq, 1, 2, M_HEADS, M_DQK, M_DV), F32),
            jax.ShapeDtypeStruct((nseq, 1, 2, M_HEADS, 1, M_DQK), F32),
            jax.ShapeDtypeStruct((nseq, 1, 2, M_HEADS, 1, 1), F32),
        ]
    return pl.pallas_call(
        functools.partial(_scan_kernel, seq, state_out),
        grid=(nseq, M_HEADS),
        in_specs=in_specs,
        out_specs=out_specs,
        out_shape=out_shape,
        scratch_shapes=[
            pltpu.VMEM((seq, M_DQK), F32),
            pltpu.VMEM((seq, M_DQK), F32),
            pltpu.VMEM((seq, M_DV), F32),
            pltpu.VMEM((seq, M_DV), F32),
            pltpu.VMEM((M_DQK, M_DV), F32),
            pltpu.VMEM((M_DQK, M_DV), F32),
        ],
        compiler_params=_cparams("arbitrary", "arbitrary"),
        name="mlstm_scan_ctx" if state_out else "mlstm_scan_smp",
    )(p_main, p_main, p_main, gates, conv_w, conv_w, conv_b, conv_b, c0, n0, m0)


def _mout_kernel(x_ref, m_ref, hn_ref, o_ref, hg_ref, w_ref, y_ref):
    t = hn_ref[...] * hg_ref[...] * jax.nn.sigmoid(o_ref[...])
    y_ref[...] = x_ref[...] + m_ref[0, 2:3, :] * _dg(t.astype(BF16), w_ref[...])


def _mlstm_out(x, mods, cmap, hn, p_main, head_g, w_out):
    ntok = x.shape[0]
    tb = PROJ_TB
    return pl.pallas_call(
        _mout_kernel,
        grid=(ntok // tb,),
        in_specs=[
            pl.BlockSpec((tb, D_MODEL), lambda i: (i, 0)),
            pl.BlockSpec((1, 6, D_MODEL), cmap),
            pl.BlockSpec((tb, M_V_DIM), lambda i: (i, 0)),
            pl.BlockSpec((tb, M_V_DIM), lambda i: (i, (M_QK_DIM + M_V_DIM) // M_V_DIM)),
            pl.BlockSpec((1, M_V_DIM), lambda i: (0, 0)),
            pl.BlockSpec((M_V_DIM, D_MODEL), lambda i: (0, 0)),
        ],
        out_specs=pl.BlockSpec((tb, D_MODEL), lambda i: (i, 0)),
        out_shape=jax.ShapeDtypeStruct((ntok, D_MODEL), F32),
        compiler_params=_cparams("arbitrary"),
        name="mlstm_out",
    )(x, mods, hn, p_main, head_g, w_out)


def _aout_kernel(x_ref, m_ref, a_ref, w_ref, y_ref):
    y_ref[...] = x_ref[...] + m_ref[0, 2:3, :] * _dg(a_ref[...].astype(BF16), w_ref[...])


def _attn_out(x, mods, cmap, att, w_out):
    ntok = x.shape[0]
    tb = PROJ_TB
    return pl.pallas_call(
        _aout_kernel,
        grid=(ntok // tb,),
        in_specs=[
            pl.BlockSpec((tb, D_MODEL), lambda i: (i, 0)),
            pl.BlockSpec((1, 6, D_MODEL), cmap),
            pl.BlockSpec((tb, D_MODEL), lambda i: (i, 0)),
            pl.BlockSpec((D_MODEL, D_MODEL), lambda i: (0, 0)),
        ],
        out_specs=pl.BlockSpec((tb, D_MODEL), lambda i: (i, 0)),
        out_shape=jax.ShapeDtypeStruct((ntok, D_MODEL), F32),
        compiler_params=_cparams("arbitrary"),
        name="attn_out",
    )(x, mods, att, w_out)


def _aproj_kernel(rope, cache_out, x_ref, m_ref, g_ref, w_ref, gs_ref, qg_ref, kg_ref, *rest):
    rest = list(rest)
    if rope:
        cos_ref, sin_ref = rest[:2]
        rest = rest[2:]
    q_ref, k_ref, v_ref = rest[:3]
    rest = rest[3:]
    h = _modulate(x_ref[...], g_ref[...], m_ref[0, 0:1, :], m_ref[0, 1:2, :])
    p = _dg(h.astype(BF16), w_ref[...])
    nq = A_HEADS * A_HEAD_DIM
    nk = 2 * A_KV_HEADS * A_HEAD_DIM
    gs = gs_ref[...]
    lane = lax.broadcasted_iota(jnp.int32, (1, LANES), 1)
    first_half = (lane % A_ROPE_AXIS) < (A_ROPE_AXIS // 2)

    def norm_rope(xs, gain):
        sq = xs * xs
        s1 = sq.astype(BF16)
        r1 = sq - s1.astype(F32)
        s2 = r1.astype(BF16)
        s3 = (r1 - s2.astype(F32)).astype(BF16)
        ss = _dg(s1, gs) + (_dg(s2, gs) + _dg(s3, gs))
        y = xs * lax.rsqrt(ss * (1.0 / A_HEAD_DIM) + EPS) * gain
        if rope:
            half = A_ROPE_AXIS // 2
            swapped = jnp.where(first_half, pltpu.roll(y, LANES - half, 1), pltpu.roll(y, half, 1))
            y = y * cos_ref[...] + swapped * sin_ref[...]
        return y

    for s in range(nq // LANES):
        q_ref[:, s * LANES:(s + 1) * LANES] = norm_rope(p[:, s * LANES:(s + 1) * LANES], qg_ref[...])
    for s in range(nk // LANES):
        kn = norm_rope(p[:, nq + s * LANES:nq + (s + 1) * LANES], kg_ref[...])
        k_ref[:, s * LANES:(s + 1) * LANES] = kn
        vs = p[:, nq + nk + s * LANES:nq + nk + (s + 1) * LANES]
        v_ref[:, s * LANES:(s + 1) * LANES] = vs
        if cache_out:
            kc_ref, vc_ref = rest
            kc_ref[0, 0, s] = kn[:, :A_HEAD_DIM]
            vc_ref[0, 0, s] = vs[:, :A_HEAD_DIM]


def _attn_project(x, mods, cmap, g, w_qkv2, gsum, qg, kg, rope_tabs, nseq, seq, cache_out):
    ntok = x.shape[0]
    tb = PROJ_TB
    nq = A_HEADS * A_HEAD_DIM
    nk = 2 * A_KV_HEADS * A_HEAD_DIM
    per = seq // tb
    rope = rope_tabs is not None
    in_specs = [
        pl.BlockSpec((tb, D_MODEL), lambda i: (i, 0)),
        pl.BlockSpec((1, 6, D_MODEL), cmap),
        pl.BlockSpec((1, D_MODEL), lambda i: (0, 0)),
        pl.BlockSpec((D_MODEL, nq + 2 * nk), lambda i: (0, 0)),
        pl.BlockSpec((LANES, LANES), lambda i: (0, 0)),
        pl.BlockSpec((1, LANES), lambda i: (0, 0)),
        pl.BlockSpec((1, LANES), lambda i: (0, 0)),
    ]
    args = [x, mods, g, w_qkv2, gsum, qg, kg]
    if rope:
        in_specs += [pl.BlockSpec((tb, LANES), lambda i: (i % per, 0))] * 2
        args += list(rope_tabs)
    out_specs = [
        pl.BlockSpec((tb, nq), lambda i: (i, 0)),
        pl.BlockSpec((tb, nk), lambda i: (i, 0)),
        pl.BlockSpec((tb, nk), lambda i: (i, 0)),
    ]
    out_shape = [
        jax.ShapeDtypeStruct((ntok, nq), F32),
        jax.ShapeDtypeStruct((ntok, nk), F32),
        jax.ShapeDtypeStruct((ntok, nk), F32),
    ]
    if cache_out:
        assert seq == tb
        cspec = pl.BlockSpec((1, 1, A_KV_HEADS, seq, A_HEAD_DIM), lambda i: (i, 0, 0, 0, 0))
        cshape = jax.ShapeDtypeStruct((nseq, 1, A_KV_HEADS, seq, A_HEAD_DIM), F32)
        out_specs += [cspec, cspec]
        out_shape += [cshape, cshape]
    return pl.pallas_call(
        functools.partial(_aproj_kernel, rope, cache_out),
        grid=(ntok // tb,),
        in_specs=in_specs,
        out_specs=out_specs,
        out_shape=out_shape,
        compiler_params=_cparams("arbitrary"),
        name="attn_project_ctx" if cache_out else "attn_project_smp",
    )(*args)


def _attn_kernel(cached, q_ref, k_ref, v_ref, *rest):
    if cached:
        kc_ref, vc_ref, o_ref = rest
    else:
        (o_ref,) = rest
    lane = lax.broadcasted_iota(jnp.int32, (1, LANES), 1)
    q = q_ref[...] * (A_HEAD_DIM ** -0.5)
    k = k_ref[...].astype(BF16)
    v = v_ref[...]
    if cached:
        kc = kc_ref[0, 0].astype(BF16)
        vc = vc_ref[0, 0]
    acc = jnp.zeros(q.shape, F32)
    for e in range(2):
        sel = (lane < A_HEAD_DIM) if e == 0 else (lane >= A_HEAD_DIM)
        qe = jnp.where(sel, q, 0.0).astype(BF16)
        s = _dg(qe, k, 1, 1)
        mx = jnp.max(s, axis=1, keepdims=True)
        if cached:
            sc = _dg(qe, kc, 1, 1)
            mx = jnp.maximum(mx, jnp.max(sc, axis=1, keepdims=True))
        p = jnp.exp(s - mx)
        l = jnp.sum(p, axis=1, keepdims=True)
        o = _dg(p.astype(BF16), jnp.where(sel, v, 0.0).astype(BF16))
        if cached:
            pc = jnp.exp(sc - mx)
            l = l + jnp.sum(pc, axis=1, keepdims=True)
            o = o + _dg(pc.astype(BF16), jnp.where(sel, vc, 0.0).astype(BF16))
        acc = acc + o / l
    o_ref[...] = acc


def _attention(q, k2, v2, cache, nseq, seq):
    ntok = q.shape[0]
    qb = min(ATTN_QB, seq)
    nqb = seq // qb
    npair = A_HEADS // 2
    in_specs = [
        pl.BlockSpec((qb, LANES), lambda b, hp, j: (b * nqb + j, hp)),
        pl.BlockSpec((seq, LANES), lambda b, hp, j: (b, hp // 2)),
        pl.BlockSpec((seq, LANES), lambda b, hp, j: (b, hp // 2)),
    ]
    args = [q, k2, v2]
    if cache is not None:
        past = cache[0].shape[2]
        cspec = pl.BlockSpec((1, 1, past, LANES), lambda b, hp, j: (b, hp // 2, 0, 0))
        in_specs += [cspec, cspec]
        args += list(cache)
    return pl.pallas_call(
        functools.partial(_attn_kernel, cache is not None),
        grid=(nseq, npair, nqb),
        in_specs=in_specs,
        out_specs=pl.BlockSpec((qb, LANES), lambda b, hp, j: (b * nqb + j, hp)),
        out_shape=jax.ShapeDtypeStruct((ntok, A_HEADS * A_HEAD_DIM), F32),
        compiler_params=_cparams("arbitrary", "arbitrary", "arbitrary"),
        name="attention_smp" if cache is not None else "attention_ctx",
    )(*args)


def _batcher_pairs(n):
    pairs = []
    p = 1
    while p < n:
        k = p
        while k >= 1:
            for j in range(k % p, n - k, 2 * k):
                for i in range(min(k, n - j - k)):
                    if (i + j) // (2 * p) == (i + j + k) // (2 * p):
                        pairs.append((i + j, i + j + k))
            k //= 2
        p *= 2
    return pairs


def _sort_levels(levels):
    lv = list(levels)
    for i, j in _batcher_pairs(len(lv)):
        lv[i], lv[j] = jnp.maximum(lv[i], lv[j]), jnp.minimum(lv[i], lv[j])
    return lv


def _pop_top(levels, k, sub):
    lv = list(levels)
    n = len(lv)
    outs = []
    for it in range(k):
        head = lv[0]
        m = jnp.max(head, axis=0, keepdims=True)
        outs.append(m)
        rem = k - 1 - it
        if rem == 0:
            break
        first = jnp.min(jnp.where(head == m, sub, 8.0), axis=0, keepdims=True)
        pop = sub == first
        for q in range(min(n, rem)):
            nxt = lv[q + 1] if q + 1 < n else -jnp.inf
            lv[q] = jnp.where(pop, nxt, lv[q])
    return outs


def _cand_levels(a, b, sub):
    k1 = P_TOPK + 1
    acol = jnp.where(sub == 0.0, a[0], jnp.where(sub == 1.0, a[1], jnp.where(sub == 2.0, a[2], a[3])))
    bcol = jnp.where(sub == 4.0, b[0], jnp.where(sub == 5.0, b[1], b[2]))
    low = sub < 4.0
    levels = []
    for lvl in range(k1):
        na = min(4, k1 // (lvl + 1))
        nb = min(3, k1 // (lvl + 5)) if lvl + 4 < k1 else 0
        valid = sub < float(na)
        x = acol + b[lvl]
        if nb:
            x = jnp.where(low, x, a[lvl + 4] + bcol)
            valid = valid | ((sub >= 4.0) & (sub < float(4 + nb)))
        levels.append(jnp.where(valid, x, -jnp.inf))
    return levels


def _pproj_kernel(x_ref, m_ref, g_ref, wqh_ref, wql_ref, skh_ref, skl_ref,
                  ht_ref, thr_ref, e1_ref, s2_ref):
    h = _modulate(x_ref[...], g_ref[...], m_ref[0, 3:4, :], m_ref[0, 4:5, :])
    ht_ref[...] = h.T.astype(BF16)
    q = _mm3w(h, wqh_ref[...], wql_ref[...])
    k1 = P_TOPK + 1
    sub = lax.broadcasted_iota(jnp.int32, (8, LANES), 0).astype(F32)
    for p in range(P_HEADS):
        sc = []
        for hf in range(2):
            ph = 2 * p + hf
            qh, ql = _split2(q[:, ph * LANES:(ph + 1) * LANES])
            kh = skh_ref[ph]
            sc.append(_dg(kh, qh, 1, 1) + (_dg(kh, ql, 1, 1) + _dg(skl_ref[ph], qh, 1, 1)))
        for lt in range(q.shape[0] // LANES):
            ls = slice(lt * LANES, (lt + 1) * LANES)
            s1 = sc[0][:, ls]
            s2 = sc[1][:, ls]
            a = _pop_top(_sort_levels([s1[8 * r:8 * r + 8] for r in range(P_NKEYS // 8)]), k1, sub)
            b = _pop_top(_sort_levels([s2[8 * r:8 * r + 8] for r in range(P_NKEYS // 8)]), k1, sub)
            v = _pop_top(_cand_levels(a, b, sub), k1, sub)
            tau = 0.5 * (v[P_TOPK - 1] + v[P_TOPK])
            z = jnp.ones_like(tau)
            for kk in range(1, P_TOPK):
                z = z + jnp.exp(v[kk] - v[0])
            thr_ref[p, :, ls] = ((tau - b[0]) - s1) * LOG2E
            e1_ref[p, :, ls] = jnp.exp(s1 - a[0]) * (0.5 / z)
            s2_ref[p, :, ls] = (s2 - b[0]) * LOG2E


def _peer_project(x, mods, cmap, g, wq_hi, wq_lo, sk_hi, sk_lo):
    ntok = x.shape[0]
    tb = PROJ_TB
    nq = P_HEADS * P_DKEY
    sel_spec = pl.BlockSpec((P_HEADS, P_NKEYS, tb), lambda i: (0, 0, i))
    sel_shape = jax.ShapeDtypeStruct((P_HEADS, P_NKEYS, ntok), F32)
    return pl.pallas_call(
        _pproj_kernel,
        grid=(ntok // tb,),
        in_specs=[
            pl.BlockSpec((tb, D_MODEL), lambda i: (i, 0)),
            pl.BlockSpec((1, 6, D_MODEL), cmap),
            pl.BlockSpec((1, D_MODEL), lambda i: (0, 0)),
            pl.BlockSpec((D_MODEL, nq), lambda i: (0, 0)),
            pl.BlockSpec((D_MODEL, nq), lambda i: (0, 0)),
            pl.BlockSpec((2 * P_HEADS, P_NKEYS, P_DKEY // 2), lambda i: (0, 0, 0)),
            pl.BlockSpec((2 * P_HEADS, P_NKEYS, P_DKEY // 2), lambda i: (0, 0, 0)),
        ],
        out_specs=[pl.BlockSpec((D_MODEL, tb), lambda i: (0, i)), sel_spec, sel_spec, sel_spec],
        out_shape=[jax.ShapeDtypeStruct((D_MODEL, ntok), BF16), sel_shape, sel_shape, sel_shape],
        compiler_params=_cparams("arbitrary"),
        name="peer_project",
    )(x, mods, g, wq_hi, wq_lo, sk_hi, sk_lo)


def _pdense_kernel(final, ne, x_ref, m_ref, ht_ref, thr_ref, e1_ref, s2_ref, u_ref, vt_ref, fg_ref,
                   y_ref, act0_s, act1_s, gt0_s, gt1_s, acc_s):
    s = pl.program_id(0)
    nchunk = pl.num_programs(0) - 2
    jc = jnp.clip(s - 2, 0, nchunk - 1) % ne

    @pl.when(s == 0)
    def _():
        act1_s[...] = jnp.zeros(act1_s.shape, F32)
        gt1_s[...] = jnp.zeros(gt1_s.shape, BF16)
        acc_s[...] = jnp.zeros(acc_s.shape, F32)

    @pl.when(s % 2 == 0)
    def _():
        _pdense_step(u_ref, ht_ref, thr_ref, e1_ref, s2_ref, vt_ref, act0_s, act1_s, gt0_s, gt1_s, acc_s)

    @pl.when(s % 2 == 1)
    def _():
        _pdense_step(u_ref, ht_ref, thr_ref, e1_ref, s2_ref, vt_ref, act1_s, act0_s, gt1_s, gt0_s, acc_s)

    @pl.when((s >= 2) & (jc == ne - 1))
    def _():
        xn = x_ref[...] + m_ref[0, 5:6, :] * acc_s[...].T
        if final:
            xn = xn * lax.rsqrt(jnp.mean(xn * xn, axis=-1, keepdims=True) + EPS) * fg_ref[...]
        y_ref[...] = xn
        acc_s[...] = jnp.zeros(acc_s.shape, F32)


def _pdense_step(u_ref, ht_ref, thr_ref, e1_ref, s2_ref, vt_ref, act_w, act_r, gt_w, gt_r, acc_s):
    ec, td = act_w.shape
    na, nb = 4, 32
    half = td // 2

    def half_body(hh, carry):
        h0 = hh * half
        hs = pl.ds(pl.multiple_of(h0, half), half)
        act_w[:, hs] = _dg(u_ref[...], ht_ref[:, hs])
        acc_s[:, hs] += _dg(vt_ref[...], gt_r[:, hs])
        for lt in range(half // LANES):
            ls = pl.ds(pl.multiple_of(h0 + lt * LANES, LANES), LANES)
            for aq in range(ec // P_NKEYS // na):
                for bq in range(P_NKEYS // nb):
                    bs = slice(bq * nb, (bq + 1) * nb)
                    w = [jnp.zeros((nb, LANES), F32) for _ in range(na)]
                    for p in range(P_HEADS):
                        s2 = s2_ref[p, bs, ls]
                        e2 = jnp.exp2(s2)
                        for ai in range(na):
                            al = aq * na + ai
                            sel = jnp.where(s2 >= thr_ref[p, al:al + 1, ls], e2, 0.0)
                            w[ai] = w[ai] + sel * e1_ref[p, al:al + 1, ls]
                    for ai in range(na):
                        r0 = (aq * na + ai) * P_NKEYS + bq * nb
                        xa = act_r[r0:r0 + nb, ls]
                        gel2 = xa * (1.0 + lax.erf(xa * (2.0 ** -0.5)))
                        gt_w[r0:r0 + nb, ls] = (w[ai] * gel2).astype(BF16)
        return carry

    lax.fori_loop(0, td // half, half_body, 0)


def _peer_dense(x, mods, cmap, ht, thr, e1, s2, u, vt, layer, final_g, final):
    ntok = x.shape[0]
    td = PEER_TD
    ec = PEER_EC
    ne = u.shape[1] // ec
    nchunk = (ntok // td) * ne
    ca = lambda s: jnp.minimum(s, nchunk - 1)
    cb = lambda s: jnp.clip(s - 1, 0, nchunk - 1)
    cc = lambda s: jnp.clip(s - 2, 0, nchunk - 1)
    sel_spec = pl.BlockSpec((P_HEADS, P_NKEYS, td), lambda s: (0, 0, cb(s) // ne))
    row_spec = pl.BlockSpec((P_HEADS, ec // P_NKEYS, td), lambda s: (0, cb(s) % ne, cb(s) // ne))
    return pl.pallas_call(
        functools.partial(_pdense_kernel, final, ne),
        grid=(nchunk + 2,),
        in_specs=[
            pl.BlockSpec((td, D_MODEL), lambda s: (cc(s) // ne, 0)),
            pl.BlockSpec((1, 6, D_MODEL), lambda s: cmap(cc(s) // ne)),
            pl.BlockSpec((D_MODEL, td), lambda s: (0, ca(s) // ne)),
            row_spec, row_spec, sel_spec,
            pl.BlockSpec((None, ec, D_MODEL), lambda s: (layer, ca(s) % ne, 0)),
            pl.BlockSpec((None, D_MODEL, ec), lambda s: (layer, 0, cc(s) % ne)),
            pl.BlockSpec((1, D_MODEL), lambda s: (0, 0)),
        ],
        out_specs=pl.BlockSpec((td, D_MODEL), lambda s: (cc(s) // ne, 0)),
        out_shape=jax.ShapeDtypeStruct((ntok, D_MODEL), F32),
        scratch_shapes=[
            pltpu.VMEM((ec, td), F32),
            pltpu.VMEM((ec, td), F32),
            pltpu.VMEM((ec, td), BF16),
            pltpu.VMEM((ec, td), BF16),
            pltpu.VMEM((D_MODEL, td), F32),
        ],
        compiler_params=_cparams("arbitrary"),
        name="peer_dense",
    )(x, mods, ht, thr, e1, s2, u, vt, final_g)


def _rope_tables(seq):
    n_rows = seq // GRID_W
    rows = jnp.repeat(jnp.arange(n_rows, dtype=F32), GRID_W)
    cols = jnp.tile(jnp.arange(GRID_W, dtype=F32), n_rows)
    half = A_ROPE_AXIS // 2
    inv_freq = ROPE_THETA ** (-jnp.arange(half, dtype=F32) / half)
    ar = rows[:, None] * inv_freq
    ac = cols[:, None] * inv_freq
    cos = jnp.concatenate([jnp.cos(ar), jnp.cos(ar), jnp.cos(ac), jnp.cos(ac)], axis=-1)
    sin = jnp.concatenate([-jnp.sin(ar), jnp.sin(ar), -jnp.sin(ac), jnp.sin(ac)], axis=-1)
    return jnp.tile(cos, (1, LANES // A_HEAD_DIM)), jnp.tile(sin, (1, LANES // A_HEAD_DIM))


def _hi_lo(w):
    hi = w.astype(BF16)
    return hi, (w - hi.astype(F32)).astype(BF16)


def kernel(x_prompt, x_sample, state_mlstm_C, state_mlstm_n, state_mlstm_m, cache_attn_k, cache_attn_v, c, c_ctx, ada_w, ada_b, norm_g, final_g, mlstm_w_in, mlstm_conv_w, mlstm_conv_b, mlstm_gate_b, mlstm_head_g, mlstm_w_out, attn_w_qkv, attn_q_g, attn_k_g, attn_w_out, peer_w_q, peer_subkeys, peer_u, peer_v):
    nb, seq_c, d = x_prompt.shape
    ndb, seq_s, _ = x_sample.shape
    assert d == D_MODEL and seq_c % PROJ_TB == 0 and seq_s % PROJ_TB == 0
    assert (nb * seq_c) % PEER_TD == 0 and seq_s % PEER_TD == 0

    nrow = -(-(1 + ndb) // 8) * 8
    cond = jnp.concatenate([c_ctx[None, :], c, jnp.zeros((nrow - 1 - ndb, d), F32)], axis=0)
    mods_all = _ada_mods(cond, ada_w, ada_b)

    groups = [
        dict(x=x_prompt.reshape(nb * seq_c, d), nseq=nb, seq=seq_c, ctx=True),
        dict(x=x_sample.reshape(ndb * seq_s, d), nseq=ndb, seq=seq_s, ctx=False),
    ]

    def cmap(gr, tb):
        return _cond_map(0, None) if gr["ctx"] else _cond_map(1, gr["seq"] // tb)

    outs = {}
    fg = final_g.reshape(1, d)

    w_in = mlstm_w_in[0]
    w_main = w_in[:, :M_MAIN_DIM].astype(BF16)
    perm = jnp.arange(4 * M_HEADS).reshape(4, M_HEADS).T.reshape(-1)
    wg_t = w_in[:, M_MAIN_DIM:].T[perm]
    wg_hi, wg_lo = _hi_lo(wg_t)
    gate_b = mlstm_gate_b[0][perm].reshape(-1, 1)
    w_mout = mlstm_w_out[0].astype(BF16)
    for gr in groups:
        nseq, seq = gr["nseq"], gr["seq"]
        mods = mods_all[0]
        p_main, gt = _mlstm_project(gr["x"], mods, cmap(gr, PROJ_TB), norm_g[0, 0].reshape(1, d),
                                    w_main, wg_hi, wg_lo, gate_b)
        nc = seq // M_CHUNK
        gates = gt.reshape(M_HEADS, 4, nseq * nc, M_CHUNK).transpose(0, 2, 1, 3)
        if gr["ctx"]:
            c0 = jnp.zeros((nseq, 1, 2, M_HEADS, M_DQK, M_DV), F32)
            n0 = jnp.zeros((nseq, 1, 2, M_HEADS, 1, M_DQK), F32)
            m0 = jnp.zeros((nseq, 1, 2, M_HEADS, 1, 1), F32)
        else:
            c0 = state_mlstm_C.astype(F32)
            n0 = state_mlstm_n.astype(F32).reshape(nseq, -1, 2, M_HEADS, 1, M_DQK)
            m0 = state_mlstm_m.astype(F32).reshape(nseq, -1, 2, M_HEADS, 1, 1)
        res = _mlstm_scan(p_main, gates, mlstm_conv_w[0], mlstm_conv_b[0].reshape(1, -1),
                          c0, n0, m0, nseq, seq, gr["ctx"])
        if gr["ctx"]:
            hn, c_new, n_new, m_new = res
            outs["C"] = c_new
            outs["n"] = n_new.reshape(nseq, 1, 2, M_HEADS, M_DQK)
            outs["m"] = m_new.reshape(nseq, 1, 2, M_HEADS)
        else:
            (hn,) = res
        gr["x"] = _mlstm_out(gr["x"], mods, cmap(gr, PROJ_TB), hn, p_main,
                             mlstm_head_g[0].reshape(1, -1), w_mout)

    def peer(layer, final):
        wq_hi, wq_lo = _hi_lo(peer_w_q[layer])
        sk_hi, sk_lo = _hi_lo(peer_subkeys[layer].reshape(2 * P_HEADS, P_NKEYS, P_DKEY // 2))
        for gr in groups:
            mods = mods_all[layer]
            ht, thr, e1, s2 = _peer_project(gr["x"], mods, cmap(gr, PROJ_TB), norm_g[layer, 1].reshape(1, d),
                                            wq_hi, wq_lo, sk_hi, sk_lo)
            gr["x"] = _peer_dense(gr["x"], mods, cmap(gr, PEER_TD), ht, thr, e1, s2, u_all, vt_all, layer, fg, final)

    u_all = peer_u.astype(BF16)
    vt_all = jnp.swapaxes(peer_v.astype(BF16), 1, 2)
    peer(0, False)

    wqkv = attn_w_qkv[0]
    nq = A_HEADS * A_HEAD_DIM
    nkv = A_KV_HEADS * A_HEAD_DIM
    dup = lambda w: jnp.tile(w.reshape(d, A_KV_HEADS, 1, A_HEAD_DIM), (1, 1, 2, 1)).reshape(d, 2 * nkv)
    w_qkv2 = jnp.concatenate([wqkv[:, :nq], dup(wqkv[:, nq:nq + nkv]), dup(wqkv[:, nq + nkv:])], axis=1).astype(BF16)
    li = jnp.arange(LANES)
    gsum = (li[:, None] // A_HEAD_DIM == li[None, :] // A_HEAD_DIM).astype(BF16)
    qg = jnp.tile(attn_q_g[0], LANES // A_HEAD_DIM).reshape(1, LANES)
    kg = jnp.tile(attn_k_g[0], LANES // A_HEAD_DIM).reshape(1, LANES)
    w_aout = attn_w_out[0].astype(BF16)
    for gr in groups:
        nseq, seq = gr["nseq"], gr["seq"]
        mods = mods_all[1]
        tabs = None if gr["ctx"] else _rope_tables(seq)
        res = _attn_project(gr["x"], mods, cmap(gr, PROJ_TB), norm_g[1, 0].reshape(1, d), w_qkv2, gsum, qg, kg,
                            tabs, nseq, seq, gr["ctx"])
        if gr["ctx"]:
            q, k2, v2, kc_new, vc_new = res
            outs["k"], outs["v"] = kc_new, vc_new
            cache = None
        else:
            q, k2, v2 = res
            cache = (jnp.tile(cache_attn_k[:, 0].astype(F32), (1, 1, 1, 2)),
                     jnp.tile(cache_attn_v[:, 0].astype(F32), (1, 1, 1, 2)))
        att = _attention(q, k2, v2, cache, nseq, seq)
        gr["x"] = _attn_out(gr["x"], mods, cmap(gr, PROJ_TB), att, w_aout)

    peer(1, True)

    y_prompt = groups[0]["x"].reshape(nb, seq_c, d)
    y_sample = groups[1]["x"].reshape(ndb, seq_s, d)
    return (y_prompt, y_sample, outs["C"], outs["n"], outs["m"], outs["k"], outs["v"])
```

```python
import functools

import jax
import jax.numpy as jnp
from jax import lax
from jax.experimental import pallas as pl
from jax.experimental.pallas import tpu as pltpu

F32 = jnp.float32
BF16 = jnp.bfloat16

EPS = 1e-6
LOG2E = 1.4426950408889634
D_MODEL = 1024
GRID_W = 64
ROPE_THETA = 10000.0

M_HEADS = 4
M_DQK = 128
M_DV = 256
M_CHUNK = 128
M_QK_DIM = 2 * M_HEADS * M_DQK
M_V_DIM = M_HEADS * M_DV
M_MAIN_DIM = M_QK_DIM + 2 * M_V_DIM

A_HEADS = 16
A_KV_HEADS = 4
A_HEAD_DIM = 64
A_ROPE_AXIS = A_HEAD_DIM // 2

P_HEADS = 8
P_NKEYS = 128
P_DKEY = 256
P_TOPK = 16

LANES = 128
PROJ_TB = 256
PEER_TB = 256
PEER_TD = 512
PEER_EC = 2048
ATTN_QB = 512
SCAN_HEADS = 2
VMEM_LIMIT = 56 * 1024 * 1024


def _cparams(*sem):
    return pltpu.CompilerParams(dimension_semantics=sem, vmem_limit_bytes=VMEM_LIMIT)


def _dg(a, b, ca=1, cb=0):
    return lax.dot_general(a, b, (((ca,), (cb,)), ((), ())), preferred_element_type=F32)


def _split2(x):
    hi = x.astype(BF16)
    lo = (x - hi.astype(F32)).astype(BF16)
    return hi, lo


def _mm3(a, b, ca=1, cb=0):
    ah, al = _split2(a)
    bh, bl = _split2(b)
    return _dg(ah, bh, ca, cb) + (_dg(ah, bl, ca, cb) + _dg(al, bh, ca, cb))


def _mm3w(a, wh, wl, ca=1, cb=0):
    ah, al = _split2(a)
    return _dg(ah, wh, ca, cb) + (_dg(ah, wl, ca, cb) + _dg(al, wh, ca, cb))


def _modulate(x, g, shift, scale):
    y = x * lax.rsqrt(jnp.mean(x * x, axis=-1, keepdims=True) + EPS)
    return (y * g) * (1.0 + scale) + shift


def _cond_map(base, per):
    if per is None:
        return lambda i, *_: (base, 0, 0)
    return lambda i, *_: (base + i // per, 0, 0)


def _ada_kernel(c_ref, w_ref, b_ref, o_ref):
    c = c_ref[...]
    a = c * jax.nn.sigmoid(c)
    o_ref[0] = _mm3(a, w_ref[0]) + b_ref[0]


def _ada_mods(cond, ada_w, ada_b):
    depth, d, n = ada_w.shape
    tn = 512
    rows = cond.shape[0]
    out = pl.pallas_call(
        _ada_kernel,
        grid=(depth, n // tn),
        in_specs=[
            pl.BlockSpec((rows, d), lambda l, j: (0, 0)),
            pl.BlockSpec((1, d, tn), lambda l, j: (l, 0, j)),
            pl.BlockSpec((1, 1, tn), lambda l, j: (l, 0, j)),
        ],
        out_specs=pl.BlockSpec((1, rows, tn), lambda l, j: (l, 0, j)),
        out_shape=jax.ShapeDtypeStruct((depth, rows, n), F32),
        compiler_params=_cparams("arbitrary", "arbitrary"),
        name="ada_mods",
    )(cond, ada_w, ada_b.reshape(depth, 1, n))
    return out.reshape(depth, rows, 6, d)


def _mproj_kernel(x_ref, m_ref, g_ref, w_ref, wgh_ref, wgl_ref, gb_ref, p_ref, gt_ref):
    h = _modulate(x_ref[...], g_ref[...], m_ref[0, 0:1, :], m_ref[0, 1:2, :])
    p_ref[...] = _dg(h.astype(BF16), w_ref[...])
    hh, hl = _split2(h)
    wh = wgh_ref[...]
    gt = _dg(wh, hh, 1, 1) + (_dg(wh, hl, 1, 1) + _dg(wgl_ref[...], hh, 1, 1))
    gt_ref[...] = gt + gb_ref[...]


def _mlstm_project(x, mods, cmap, g, w_main, wg_hi, wg_lo, gate_b):
    ntok = x.shape[0]
    tb = PROJ_TB
    ng = wg_hi.shape[0]
    return pl.pallas_call(
        _mproj_kernel,
        grid=(ntok // tb,),
        in_specs=[
            pl.BlockSpec((tb, D_MODEL), lambda i: (i, 0)),
            pl.BlockSpec((1, 6, D_MODEL), cmap),
            pl.BlockSpec((1, D_MODEL), lambda i: (0, 0)),
            pl.BlockSpec((D_MODEL, M_MAIN_DIM), lambda i: (0, 0)),
            pl.BlockSpec((ng, D_MODEL), lambda i: (0, 0)),
            pl.BlockSpec((ng, D_MODEL), lambda i: (0, 0)),
            pl.BlockSpec((ng, 1), lambda i: (0, 0)),
        ],
        out_specs=[
            pl.BlockSpec((tb, M_MAIN_DIM), lambda i: (i, 0)),
            pl.BlockSpec((ng, tb), lambda i: (0, i)),
        ],
        out_shape=[
            jax.ShapeDtypeStruct((ntok, M_MAIN_DIM), F32),
            jax.ShapeDtypeStruct((ng, ntok), F32),
        ],
        compiler_params=_cparams("arbitrary"),
        name="mlstm_project",
    )(x, mods, g, w_main, wg_hi, wg_lo, gate_b)


def _scan_kernel(seq, state_out, qp_ref, kp_ref, v_ref, g_ref, cwq_ref, cwk_ref, cbq_ref, cbk_ref,
                 c0_ref, n0_ref, m0_ref, hn_ref, *rest):
    if state_out:
        co_ref, no_ref, mo_ref, q_s, k_s, hf_s, hb_s, cf_s, cb_s = rest
    else:
        q_s, k_s, hf_s, hb_s, cf_s, cb_s = rest
    h_dir = (hf_s, hb_s)
    c_dir = (cf_s, cb_s)
    nc = seq // M_CHUNK
    nh = cf_s.shape[0]
    rows = lax.broadcasted_iota(jnp.int32, (seq, 1), 0)

    def conv(x, w_ref, b_ref):
        xm1 = jnp.where(rows == 0, 0.0, pltpu.roll(x, 1, 0))
        xp1 = jnp.where(rows == seq - 1, 0.0, pltpu.roll(x, seq - 1, 0))
        y = xm1 * w_ref[0:1, :] + x * w_ref[1:2, :] + xp1 * w_ref[2:3, :] + b_ref[...]
        return y * jax.nn.sigmoid(y)

    q_s[...] = conv(qp_ref[...], cwq_ref, cbq_ref)
    k_s[...] = conv(kp_ref[...], cwk_ref, cbk_ref) * (M_DQK ** -0.5)

    ti = lax.broadcasted_iota(jnp.int32, (M_CHUNK, M_CHUNK), 0)
    si = lax.broadcasted_iota(jnp.int32, (M_CHUNK, M_CHUNK), 1)
    eye = ti == si

    def row2col(r):
        return jnp.sum(jnp.where(eye, r, 0.0), axis=1, keepdims=True)

    def col2row(c):
        return jnp.sum(jnp.where(eye, c, 0.0), axis=0, keepdims=True)

    def chunk_step(dirn, hd, c, n, m):
        mask = (si <= ti) if dirn == 0 else (si >= ti)
        c_s = c_dir[dirn]
        qk_cols = slice(hd * M_DQK, (hd + 1) * M_DQK)
        v_cols = slice(hd * M_DV, (hd + 1) * M_DV)
        r0 = pl.multiple_of(c * M_CHUNK, M_CHUNK)
        g4 = g_ref[hd, c]
        ig = g4[2 * dirn:2 * dirn + 1, :]
        fp = g4[2 * dirn + 1:2 * dirn + 2, :]
        lf = jnp.minimum(fp, 0.0) - jnp.log1p(jnp.exp(-jnp.abs(fp)))
        b_col = jnp.sum(jnp.where(mask, lf, 0.0), axis=1, keepdims=True)
        b_row = col2row(b_col)
        b_end = jnp.sum(lf, axis=1, keepdims=True)
        qc = q_s[pl.ds(r0, M_CHUNK), qk_cols]
        kc = k_s[pl.ds(r0, M_CHUNK), qk_cols]
        vc = v_ref[pl.ds(r0, M_CHUNK), v_cols]
        d = jnp.where(mask, b_col - b_row + ig, -jnp.inf)
        inter = b_col + m
        m_t = jnp.maximum(inter, jnp.max(d, axis=1, keepdims=True))
        qb = qc.astype(BF16)
        w = jnp.exp(d - m_t) * _dg(qb, kc.astype(BF16), 1, 1)
        a = jnp.exp(inter - m_t)
        c_old = c_s[hd]
        num = a * _dg(qb, c_old.astype(BF16)) + _dg(w.astype(BF16), vc.astype(BF16))
        den = a * jnp.sum(qc * n, axis=1, keepdims=True) + jnp.sum(w, axis=1, keepdims=True)
        h_dir[dirn][pl.ds(r0, M_CHUNK), v_cols] = num / jnp.maximum(jnp.abs(den), jnp.exp(-m_t))
        to_end = b_end - b_row + ig
        m_new = jnp.maximum(b_end + m, jnp.max(to_end, axis=1, keepdims=True))
        decay = jnp.exp(b_end + m - m_new)
        kw = kc * row2col(jnp.exp(to_end - m_new))
        c_s[hd] = decay * c_old + _mm3(kw.T, vc)
        n_new = decay * n + jnp.sum(kw, axis=0, keepdims=True)
        return n_new, m_new

    chains = [(dirn, hd) for hd in range(nh) for dirn in range(2)]
    for dirn, hd in chains:
        c_dir[dirn][hd] = c0_ref[0, 0, dirn, hd]

    def body(it, carry):
        out = []
        for i, (dirn, hd) in enumerate(chains):
            c = it if dirn == 0 else nc - 1 - it
            out.extend(chunk_step(dirn, hd, c, carry[2 * i], carry[2 * i + 1]))
        return tuple(out)

    init = []
    for dirn, hd in chains:
        init.extend((n0_ref[0, 0, dirn, hd], m0_ref[0, 0, dirn, hd]))
    fin = lax.fori_loop(0, nc, body, tuple(init))
    if state_out:
        for i, (dirn, hd) in enumerate(chains):
            co_ref[0, 0, dirn, hd] = c_dir[dirn][hd]
            no_ref[0, 0, dirn, hd] = fin[2 * i]
            mo_ref[0, 0, dirn, hd] = fin[2 * i + 1]

    for hd in range(nh):
        v_cols = slice(hd * M_DV, (hd + 1) * M_DV)
        hm = hf_s[:, v_cols] + hb_s[:, v_cols]
        hn_ref[:, v_cols] = hm * lax.rsqrt(jnp.mean(hm * hm, axis=-1, keepdims=True) + EPS)


def _mlstm_scan(p_main, gates, conv_w, conv_b, c0, n0, m0, nseq, seq, state_out):
    ntok = nseq * seq
    nc = seq // M_CHUNK
    hps = SCAN_HEADS
    ngrp = M_HEADS // hps
    qblk = hps * M_DQK
    vblk = hps * M_DV
    st5 = lambda b, h: (b, 0, 0, h, 0, 0)
    in_specs = [
        pl.BlockSpec((seq, qblk), lambda b, h: (b, h)),
        pl.BlockSpec((seq, qblk), lambda b, h: (b, ngrp + h)),
        pl.BlockSpec((seq, vblk), lambda b, h: (b, M_QK_DIM // vblk + h)),
        pl.BlockSpec((hps, nc, 4, M_CHUNK), lambda b, h: (h, b, 0, 0)),
        pl.BlockSpec((3, qblk), lambda b, h: (0, h)),
        pl.BlockSpec((3, qblk), lambda b, h: (0, ngrp + h)),
        pl.BlockSpec((1, qblk), lambda b, h: (0, h)),
        pl.BlockSpec((1, qblk), lambda b, h: (0, ngrp + h)),
        pl.BlockSpec((1, 1, 2, hps, M_DQK, M_DV), st5),
        pl.BlockSpec((1, 1, 2, hps, 1, M_DQK), st5),
        pl.BlockSpec((1, 1, 2, hps, 1, 1), st5),
    ]
    out_specs = [pl.BlockSpec((seq, vblk), lambda b, h: (b, h))]
    out_shape = [jax.ShapeDtypeStruct((ntok, M_V_DIM), F32)]
    if state_out:
        out_specs += [
            pl.BlockSpec((1, 1, 2, hps, M_DQK, M_DV), st5),
            pl.BlockSpec((1, 1, 2, hps, 1, M_DQK), st5),
            pl.BlockSpec((1, 1, 2, hps, 1, 1), st5),
        ]
        out_shape += [
            jax.ShapeDtypeStruct((nseq, 1, 2, M_HEADS, M_DQK, M_DV), F32),
            jax.ShapeDtypeStruct((nseq, 1, 2, M_HEADS, 1, M_DQK), F32),
            jax.ShapeDtypeStruct((nseq, 1, 2, M_HEADS, 1, 1), F32),
        ]
    return pl.pallas_call(
        functools.partial(_scan_kernel, seq, state_out),
        grid=(nseq, ngrp),
        in_specs=in_specs,
        out_specs=out_specs,
        out_shape=out_shape,
        scratch_shapes=[
            pltpu.VMEM((seq, qblk), F32),
            pltpu.VMEM((seq, qblk), F32),
            pltpu.VMEM((seq, vblk), F32),
            pltpu.VMEM((seq, vblk), F32),
            pltpu.VMEM((hps, M_DQK, M_DV), F32),
            pltpu.VMEM((hps, M_DQK, M_DV), F32),
        ],
        compiler_params=_cparams("arbitrary", "arbitrary"),
        name="mlstm_scan_ctx" if state_out else "mlstm_scan_smp",
    )(p_main, p_main, p_main, gates, conv_w, conv_w, conv_b, conv_b, c0, n0, m0)


def _mout_kernel(x_ref, m_ref, hn_ref, o_ref, hg_ref, w_ref, y_ref):
    t = hn_ref[...] * hg_ref[...] * jax.nn.sigmoid(o_ref[...])
    y_ref[...] = x_ref[...] + m_ref[0, 2:3, :] * _dg(t.astype(BF16), w_ref[...])


def _mlstm_out(x, mods, cmap, hn, p_main, head_g, w_out):
    ntok = x.shape[0]
    tb = PROJ_TB
    return pl.pallas_call(
        _mout_kernel,
        grid=(ntok // tb,),
        in_specs=[
            pl.BlockSpec((tb, D_MODEL), lambda i: (i, 0)),
            pl.BlockSpec((1, 6, D_MODEL), cmap),
            pl.BlockSpec((tb, M_V_DIM), lambda i: (i, 0)),
            pl.BlockSpec((tb, M_V_DIM), lambda i: (i, (M_QK_DIM + M_V_DIM) // M_V_DIM)),
            pl.BlockSpec((1, M_V_DIM), lambda i: (0, 0)),
            pl.BlockSpec((M_V_DIM, D_MODEL), lambda i: (0, 0)),
        ],
        out_specs=pl.BlockSpec((tb, D_MODEL), lambda i: (i, 0)),
        out_shape=jax.ShapeDtypeStruct((ntok, D_MODEL), F32),
        compiler_params=_cparams("arbitrary"),
        name="mlstm_out",
    )(x, mods, hn, p_main, head_g, w_out)


def _aout_kernel(x_ref, m_ref, a_ref, w_ref, y_ref):
    y_ref[...] = x_ref[...] + m_ref[0, 2:3, :] * _dg(a_ref[...].astype(BF16), w_ref[...])


def _attn_out(x, mods, cmap, att, w_out):
    ntok = x.shape[0]
    tb = PROJ_TB
    return pl.pallas_call(
        _aout_kernel,
        grid=(ntok // tb,),
        in_specs=[
            pl.BlockSpec((tb, D_MODEL), lambda i: (i, 0)),
            pl.BlockSpec((1, 6, D_MODEL), cmap),
            pl.BlockSpec((tb, D_MODEL), lambda i: (i, 0)),
            pl.BlockSpec((D_MODEL, D_MODEL), lambda i: (0, 0)),
        ],
        out_specs=pl.BlockSpec((tb, D_MODEL), lambda i: (i, 0)),
        out_shape=jax.ShapeDtypeStruct((ntok, D_MODEL), F32),
        compiler_params=_cparams("arbitrary"),
        name="attn_out",
    )(x, mods, att, w_out)


def _aproj_kernel(rope, cache_out, x_ref, m_ref, g_ref, w_ref, gs_ref, qg_ref, kg_ref, *rest):
    rest = list(rest)
    if rope:
        cos_ref, sin_ref = rest[:2]
        rest = rest[2:]
    q_ref, k_ref, v_ref = rest[:3]
    rest = rest[3:]
    h = _modulate(x_ref[...], g_ref[...], m_ref[0, 0:1, :], m_ref[0, 1:2, :])
    p = _dg(h.astype(BF16), w_ref[...])
    nq = A_HEADS * A_HEAD_DIM
    nk = 2 * A_KV_HEADS * A_HEAD_DIM
    gs = gs_ref[...]
    lane = lax.broadcasted_iota(jnp.int32, (1, LANES), 1)
    first_half = (lane % A_ROPE_AXIS) < (A_ROPE_AXIS // 2)

    def norm_rope(xs, gain):
        sq = xs * xs
        s1 = sq.astype(BF16)
        r1 = sq - s1.astype(F32)
        s2 = r1.astype(BF16)
        s3 = (r1 - s2.astype(F32)).astype(BF16)
        ss = _dg(s1, gs) + (_dg(s2, gs) + _dg(s3, gs))
        y = xs * lax.rsqrt(ss * (1.0 / A_HEAD_DIM) + EPS) * gain
        if rope:
            half = A_ROPE_AXIS // 2
            swapped = jnp.where(first_half, pltpu.roll(y, LANES - half, 1), pltpu.roll(y, half, 1))
            y = y * cos_ref[...] + swapped * sin_ref[...]
        return y

    for s in range(nq // LANES):
        q_ref[:, s * LANES:(s + 1) * LANES] = norm_rope(p[:, s * LANES:(s + 1) * LANES], qg_ref[...])
    for s in range(nk // LANES):
        kn = norm_rope(p[:, nq + s * LANES:nq + (s + 1) * LANES], kg_ref[...])
        k_ref[:, s * LANES:(s + 1) * LANES] = kn
        vs = p[:, nq + nk + s * LANES:nq + nk + (s + 1) * LANES]
        v_ref[:, s * LANES:(s + 1) * LANES] = vs
        if cache_out:
            kc_ref, vc_ref = rest
            kc_ref[0, 0, s] = kn[:, :A_HEAD_DIM]
            vc_ref[0, 0, s] = vs[:, :A_HEAD_DIM]


def _attn_project(x, mods, cmap, g, w_qkv2, gsum, qg, kg, rope_tabs, nseq, seq, cache_out):
    ntok = x.shape[0]
    tb = PROJ_TB
    nq = A_HEADS * A_HEAD_DIM
    nk = 2 * A_KV_HEADS * A_HEAD_DIM
    per = seq // tb
    rope = rope_tabs is not None
    in_specs = [
        pl.BlockSpec((tb, D_MODEL), lambda i: (i, 0)),
        pl.BlockSpec((1, 6, D_MODEL), cmap),
        pl.BlockSpec((1, D_MODEL), lambda i: (0, 0)),
        pl.BlockSpec((D_MODEL, nq + 2 * nk), lambda i: (0, 0)),
        pl.BlockSpec((LANES, LANES), lambda i: (0, 0)),
        pl.BlockSpec((1, LANES), lambda i: (0, 0)),
        pl.BlockSpec((1, LANES), lambda i: (0, 0)),
    ]
    args = [x, mods, g, w_qkv2, gsum, qg, kg]
    if rope:
        in_specs += [pl.BlockSpec((tb, LANES), lambda i: (i % per, 0))] * 2
        args += list(rope_tabs)
    out_specs = [
        pl.BlockSpec((tb, nq), lambda i: (i, 0)),
        pl.BlockSpec((tb, nk), lambda i: (i, 0)),
        pl.BlockSpec((tb, nk), lambda i: (i, 0)),
    ]
    out_shape = [
        jax.ShapeDtypeStruct((ntok, nq), F32),
        jax.ShapeDtypeStruct((ntok, nk), F32),
        jax.ShapeDtypeStruct((ntok, nk), F32),
    ]
    if cache_out:
        assert seq == tb
        cspec = pl.BlockSpec((1, 1, A_KV_HEADS, seq, A_HEAD_DIM), lambda i: (i, 0, 0, 0, 0))
        cshape = jax.ShapeDtypeStruct((nseq, 1, A_KV_HEADS, seq, A_HEAD_DIM), F32)
        out_specs += [cspec, cspec]
        out_shape += [cshape, cshape]
    return pl.pallas_call(
        functools.partial(_aproj_kernel, rope, cache_out),
        grid=(ntok // tb,),
        in_specs=in_specs,
        out_specs=out_specs,
        out_shape=out_shape,
        compiler_params=_cparams("arbitrary"),
        name="attn_project_ctx" if cache_out else "attn_project_smp",
    )(*args)


def _attn_kernel(cached, q_ref, k_ref, v_ref, *rest):
    if cached:
        kc_ref, vc_ref, o_ref = rest
    else:
        (o_ref,) = rest
    lane = lax.broadcasted_iota(jnp.int32, (1, LANES), 1)
    q = q_ref[...] * (A_HEAD_DIM ** -0.5)
    k = k_ref[...].astype(BF16)
    v = v_ref[...]
    if cached:
        kc = kc_ref[0, 0].astype(BF16)
        vc = vc_ref[0, 0]
    acc = jnp.zeros(q.shape, F32)
    for e in range(2):
        sel = (lane < A_HEAD_DIM) if e == 0 else (lane >= A_HEAD_DIM)
        qe = jnp.where(sel, q, 0.0).astype(BF16)
        s = _dg(qe, k, 1, 1)
        mx = jnp.max(s, axis=1, keepdims=True)
        if cached:
            sc = _dg(qe, kc, 1, 1)
            mx = jnp.maximum(mx, jnp.max(sc, axis=1, keepdims=True))
        p = jnp.exp(s - mx)
        l = jnp.sum(p, axis=1, keepdims=True)
        o = _dg(p.astype(BF16), jnp.where(sel, v, 0.0).astype(BF16))
        if cached:
            pc = jnp.exp(sc - mx)
            l = l + jnp.sum(pc, axis=1, keepdims=True)
            o = o + _dg(pc.astype(BF16), jnp.where(sel, vc, 0.0).astype(BF16))
        acc = acc + o / l
    o_ref[...] = acc


def _attention(q, k2, v2, cache, nseq, seq):
    ntok = q.shape[0]
    qb = min(ATTN_QB, seq)
    nqb = seq // qb
    npair = A_HEADS // 2
    in_specs = [
        pl.BlockSpec((qb, LANES), lambda b, hp, j: (b * nqb + j, hp)),
        pl.BlockSpec((seq, LANES), lambda b, hp, j: (b, hp // 2)),
        pl.BlockSpec((seq, LANES), lambda b, hp, j: (b, hp // 2)),
    ]
    args = [q, k2, v2]
    if cache is not None:
        past = cache[0].shape[2]
        cspec = pl.BlockSpec((1, 1, past, LANES), lambda b, hp, j: (b, hp // 2, 0, 0))
        in_specs += [cspec, cspec]
        args += list(cache)
    return pl.pallas_call(
        functools.partial(_attn_kernel, cache is not None),
        grid=(nseq, npair, nqb),
        in_specs=in_specs,
        out_specs=pl.BlockSpec((qb, LANES), lambda b, hp, j: (b * nqb + j, hp)),
        out_shape=jax.ShapeDtypeStruct((ntok, A_HEADS * A_HEAD_DIM), F32),
        compiler_params=_cparams("arbitrary", "arbitrary", "arbitrary"),
        name="attention_smp" if cache is not None else "attention_ctx",
    )(*args)


def _batcher_pairs(n):
    pairs = []
    p = 1
    while p < n:
        k = p
        while k >= 1:
            for j in range(k % p, n - k, 2 * k):
                for i in range(min(k, n - j - k)):
                    if (i + j) // (2 * p) == (i + j + k) // (2 * p):
                        pairs.append((i + j, i + j + k))
            k //= 2
        p *= 2
    return pairs


def _sort_levels(levels):
    lv = list(levels)
    for i, j in _batcher_pairs(len(lv)):
        lv[i], lv[j] = jnp.maximum(lv[i], lv[j]), jnp.minimum(lv[i], lv[j])
    return lv


def _pop_top(levels, k, sub):
    lv = list(levels)
    n = len(lv)
    outs = []
    for it in range(k):
        head = lv[0]
        m = jnp.max(head, axis=0, keepdims=True)
        outs.append(m)
        rem = k - 1 - it
        if rem == 0:
            break
        first = jnp.min(jnp.where(head == m, sub, 8.0), axis=0, keepdims=True)
        pop = sub == first
        for q in range(min(n, rem)):
            nxt = lv[q + 1] if q + 1 < n else -jnp.inf
            lv[q] = jnp.where(pop, nxt, lv[q])
    return outs


def _cand_levels(a, b, sub):
    k1 = P_TOPK + 1
    acol = jnp.where(sub == 0.0, a[0], jnp.where(sub == 1.0, a[1], jnp.where(sub == 2.0, a[2], a[3])))
    bcol = jnp.where(sub == 4.0, b[0], jnp.where(sub == 5.0, b[1], b[2]))
    low = sub < 4.0
    levels = []
    for lvl in range(k1):
        na = min(4, k1 // (lvl + 1))
        nb = min(3, k1 // (lvl + 5)) if lvl + 4 < k1 else 0
        valid = sub < float(na)
        x = acol + b[lvl]
        if nb:
            x = jnp.where(low, x, a[lvl + 4] + bcol)
            valid = valid | ((sub >= 4.0) & (sub < float(4 + nb)))
        levels.append(jnp.where(valid, x, -jnp.inf))
    return levels


def _pproj_kernel(x_ref, m_ref, g_ref, wqh_ref, wql_ref, skh_ref, skl_ref,
                  ht_ref, thr_ref, e1_ref, s2_ref):
    h = _modulate(x_ref[...], g_ref[...], m_ref[0, 3:4, :], m_ref[0, 4:5, :])
    ht_ref[...] = h.T.astype(BF16)
    q = _mm3w(h, wqh_ref[...], wql_ref[...])
    k1 = P_TOPK + 1
    sub = lax.broadcasted_iota(jnp.int32, (8, LANES), 0).astype(F32)
    for p in range(P_HEADS):
        sc = []
        for hf in range(2):
            ph = 2 * p + hf
            qh, ql = _split2(q[:, ph * LANES:(ph + 1) * LANES])
            kh = skh_ref[ph]
            sc.append(_dg(kh, qh, 1, 1) + (_dg(kh, ql, 1, 1) + _dg(skl_ref[ph], qh, 1, 1)))
        for lt in range(q.shape[0] // LANES):
            ls = slice(lt * LANES, (lt + 1) * LANES)
            s1 = sc[0][:, ls]
            s2 = sc[1][:, ls]
            a = _pop_top(_sort_levels([s1[8 * r:8 * r + 8] for r in range(P_NKEYS // 8)]), k1, sub)
            b = _pop_top(_sort_levels([s2[8 * r:8 * r + 8] for r in range(P_NKEYS // 8)]), k1, sub)
            v = _pop_top(_cand_levels(a, b, sub), k1, sub)
            tau = 0.5 * (v[P_TOPK - 1] + v[P_TOPK])
            z = jnp.ones_like(tau)
            for kk in range(1, P_TOPK):
                z = z + jnp.exp(v[kk] - v[0])
            thr_ref[p, :, ls] = ((tau - b[0]) - s1) * LOG2E
            e1_ref[p, :, ls] = jnp.exp(s1 - a[0]) * (0.5 / z)
            s2_ref[p, :, ls] = (s2 - b[0]) * LOG2E


def _peer_project(x, mods, cmap, g, wq_hi, wq_lo, sk_hi, sk_lo):
    ntok = x.shape[0]
    tb = PEER_TB
    nq = P_HEADS * P_DKEY
    sel_spec = pl.BlockSpec((P_HEADS, P_NKEYS, tb), lambda i: (0, 0, i))
    sel_shape = jax.ShapeDtypeStruct((P_HEADS, P_NKEYS, ntok), F32)
    return pl.pallas_call(
        _pproj_kernel,
        grid=(ntok // tb,),
        in_specs=[
            pl.BlockSpec((tb, D_MODEL), lambda i: (i, 0)),
            pl.BlockSpec((1, 6, D_MODEL), cmap),
            pl.BlockSpec((1, D_MODEL), lambda i: (0, 0)),
            pl.BlockSpec((D_MODEL, nq), lambda i: (0, 0)),
            pl.BlockSpec((D_MODEL, nq), lambda i: (0, 0)),
            pl.BlockSpec((2 * P_HEADS, P_NKEYS, P_DKEY // 2), lambda i: (0, 0, 0)),
            pl.BlockSpec((2 * P_HEADS, P_NKEYS, P_DKEY // 2), lambda i: (0, 0, 0)),
        ],
        out_specs=[pl.BlockSpec((D_MODEL, tb), lambda i: (0, i)), sel_spec, sel_spec, sel_spec],
        out_shape=[jax.ShapeDtypeStruct((D_MODEL, ntok), BF16), sel_shape, sel_shape, sel_shape],
        compiler_params=_cparams("arbitrary"),
        name="peer_project",
    )(x, mods, g, wq_hi, wq_lo, sk_hi, sk_lo)


def _pdense_kernel(final, ne, x_ref, m_ref, ht_ref, thr_ref, e1_ref, s2_ref, u_ref, vt_ref, fg_ref,
                   y_ref, act0_s, act1_s, gt0_s, gt1_s, acc_s):
    s = pl.program_id(0)
    nchunk = pl.num_programs(0) - 2
    jc = jnp.clip(s - 2, 0, nchunk - 1) % ne

    @pl.when(s == 0)
    def _():
        act1_s[...] = jnp.zeros(act1_s.shape, F32)
        gt1_s[...] = jnp.zeros(gt1_s.shape, BF16)
        acc_s[...] = jnp.zeros(acc_s.shape, F32)

    @pl.when(s % 2 == 0)
    def _():
        _pdense_step(u_ref, ht_ref, thr_ref, e1_ref, s2_ref, vt_ref, act0_s, act1_s, gt0_s, gt1_s, acc_s)

    @pl.when(s % 2 == 1)
    def _():
        _pdense_step(u_ref, ht_ref, thr_ref, e1_ref, s2_ref, vt_ref, act1_s, act0_s, gt1_s, gt0_s, acc_s)

    @pl.when((s >= 2) & (jc == ne - 1))
    def _():
        xn = x_ref[...] + m_ref[0, 5:6, :] * acc_s[...].T
        if final:
            xn = xn * lax.rsqrt(jnp.mean(xn * xn, axis=-1, keepdims=True) + EPS) * fg_ref[...]
        y_ref[...] = xn
        acc_s[...] = jnp.zeros(acc_s.shape, F32)


def _pdense_step(u_ref, ht_ref, thr_ref, e1_ref, s2_ref, vt_ref, act_w, act_r, gt_w, gt_r, acc_s):
    ec, td = act_w.shape
    na, nb = 4, 32
    half = td // 2

    def half_body(hh, carry):
        h0 = hh * half
        hs = pl.ds(pl.multiple_of(h0, half), half)
        act_w[:, hs] = _dg(u_ref[...], ht_ref[:, hs])
        acc_s[:, hs] += _dg(vt_ref[...], gt_r[:, hs])
        for lt in range(half // LANES):
            ls = pl.ds(pl.multiple_of(h0 + lt * LANES, LANES), LANES)
            for aq in range(ec // P_NKEYS // na):
                for bq in range(P_NKEYS // nb):
                    bs = slice(bq * nb, (bq + 1) * nb)
                    w = [jnp.zeros((nb, LANES), F32) for _ in range(na)]
                    for p in range(P_HEADS):
                        s2 = s2_ref[p, bs, ls]
                        e2 = jnp.exp2(s2)
                        for ai in range(na):
                            al = aq * na + ai
                            sel = jnp.where(s2 >= thr_ref[p, al:al + 1, ls], e2, 0.0)
                            w[ai] = w[ai] + sel * e1_ref[p, al:al + 1, ls]
                    for ai in range(na):
                        r0 = (aq * na + ai) * P_NKEYS + bq * nb
                        xa = act_r[r0:r0 + nb, ls]
                        gel2 = xa * (1.0 + lax.erf(xa * (2.0 ** -0.5)))
                        gt_w[r0:r0 + nb, ls] = (w[ai] * gel2).astype(BF16)
        return carry

    lax.fori_loop(0, td // half, half_body, 0)


def _peer_dense(x, mods, cmap, ht, thr, e1, s2, u, vt, layer, final_g, final):
    ntok = x.shape[0]
    td = PEER_TD
    ec = PEER_EC
    ne = u.shape[1] // ec
    nchunk = (ntok // td) * ne
    ca = lambda s: jnp.minimum(s, nchunk - 1)
    cb = lambda s: jnp.clip(s - 1, 0, nchunk - 1)
    cc = lambda s: jnp.clip(s - 2, 0, nchunk - 1)
    sel_spec = pl.BlockSpec((P_HEADS, P_NKEYS, td), lambda s: (0, 0, cb(s) // ne))
    row_spec = pl.BlockSpec((P_HEADS, ec // P_NKEYS, td), lambda s: (0, cb(s) % ne, cb(s) // ne))
    return pl.pallas_call(
        functools.partial(_pdense_kernel, final, ne),
        grid=(nchunk + 2,),
        in_specs=[
            pl.BlockSpec((td, D_MODEL), lambda s: (cc(s) // ne, 0)),
            pl.BlockSpec((1, 6, D_MODEL), lambda s: cmap(cc(s) // ne)),
            pl.BlockSpec((D_MODEL, td), lambda s: (0, ca(s) // ne)),
            row_spec, row_spec, sel_spec,
            pl.BlockSpec((None, ec, D_MODEL), lambda s: (layer, ca(s) % ne, 0)),
            pl.BlockSpec((None, D_MODEL, ec), lambda s: (layer, 0, cc(s) % ne)),
            pl.BlockSpec((1, D_MODEL), lambda s: (0, 0)),
        ],
        out_specs=pl.BlockSpec((td, D_MODEL), lambda s: (cc(s) // ne, 0)),
        out_shape=jax.ShapeDtypeStruct((ntok, D_MODEL), F32),
        scratch_shapes=[
            pltpu.VMEM((ec, td), F32),
            pltpu.VMEM((ec, td), F32),
            pltpu.VMEM((ec, td), BF16),
            pltpu.VMEM((ec, td), BF16),
            pltpu.VMEM((D_MODEL, td), F32),
        ],
        compiler_params=_cparams("arbitrary"),
        name="peer_dense",
    )(x, mods, ht, thr, e1, s2, u, vt, final_g)


def _rope_tables(seq):
    n_rows = seq // GRID_W
    rows = jnp.repeat(jnp.arange(n_rows, dtype=F32), GRID_W)
    cols = jnp.tile(jnp.arange(GRID_W, dtype=F32), n_rows)
    half = A_ROPE_AXIS // 2
    inv_freq = ROPE_THETA ** (-jnp.arange(half, dtype=F32) / half)
    ar = rows[:, None] * inv_freq
    ac = cols[:, None] * inv_freq
    cos = jnp.concatenate([jnp.cos(ar), jnp.cos(ar), jnp.cos(ac), jnp.cos(ac)], axis=-1)
    sin = jnp.concatenate([-jnp.sin(ar), jnp.sin(ar), -jnp.sin(ac), jnp.sin(ac)], axis=-1)
    return jnp.tile(cos, (1, LANES // A_HEAD_DIM)), jnp.tile(sin, (1, LANES // A_HEAD_DIM))


def _hi_lo(w):
    hi = w.astype(BF16)
    return hi, (w - hi.astype(F32)).astype(BF16)


def kernel(x_prompt, x_sample, state_mlstm_C, state_mlstm_n, state_mlstm_m, cache_attn_k, cache_attn_v, c, c_ctx, ada_w, ada_b, norm_g, final_g, mlstm_w_in, mlstm_conv_w, mlstm_conv_b, mlstm_gate_b, mlstm_head_g, mlstm_w_out, attn_w_qkv, attn_q_g, attn_k_g, attn_w_out, peer_w_q, peer_subkeys, peer_u, peer_v):
    nb, seq_c, d = x_prompt.shape
    ndb, seq_s, _ = x_sample.shape
    assert d == D_MODEL and seq_c % PROJ_TB == 0 and seq_s % PROJ_TB == 0
    assert (nb * seq_c) % PEER_TD == 0 and seq_s % PEER_TD == 0

    nrow = -(-(1 + ndb) // 8) * 8
    cond = jnp.concatenate([c_ctx[None, :], c, jnp.zeros((nrow - 1 - ndb, d), F32)], axis=0)
    mods_all = _ada_mods(cond, ada_w, ada_b)

    groups = [
        dict(x=x_prompt.reshape(nb * seq_c, d), nseq=nb, seq=seq_c, ctx=True),
        dict(x=x_sample.reshape(ndb * seq_s, d), nseq=ndb, seq=seq_s, ctx=False),
    ]

    def cmap(gr, tb):
        return _cond_map(0, None) if gr["ctx"] else _cond_map(1, gr["seq"] // tb)

    outs = {}
    fg = final_g.reshape(1, d)

    w_in = mlstm_w_in[0]
    w_main = w_in[:, :M_MAIN_DIM].astype(BF16)
    perm = jnp.arange(4 * M_HEADS).reshape(4, M_HEADS).T.reshape(-1)
    wg_t = w_in[:, M_MAIN_DIM:].T[perm]
    wg_hi, wg_lo = _hi_lo(wg_t)
    gate_b = mlstm_gate_b[0][perm].reshape(-1, 1)
    w_mout = mlstm_w_out[0].astype(BF16)
    for gr in groups:
        nseq, seq = gr["nseq"], gr["seq"]
        mods = mods_all[0]
        p_main, gt = _mlstm_project(gr["x"], mods, cmap(gr, PROJ_TB), norm_g[0, 0].reshape(1, d),
                                    w_main, wg_hi, wg_lo, gate_b)
        nc = seq // M_CHUNK
        gates = gt.reshape(M_HEADS, 4, nseq * nc, M_CHUNK).transpose(0, 2, 1, 3)
        if gr["ctx"]:
            c0 = jnp.zeros((nseq, 1, 2, M_HEADS, M_DQK, M_DV), F32)
            n0 = jnp.zeros((nseq, 1, 2, M_HEADS, 1, M_DQK), F32)
            m0 = jnp.zeros((nseq, 1, 2, M_HEADS, 1, 1), F32)
        else:
            c0 = state_mlstm_C.astype(F32)
            n0 = state_mlstm_n.astype(F32).reshape(nseq, -1, 2, M_HEADS, 1, M_DQK)
            m0 = state_mlstm_m.astype(F32).reshape(nseq, -1, 2, M_HEADS, 1, 1)
        res = _mlstm_scan(p_main, gates, mlstm_conv_w[0], mlstm_conv_b[0].reshape(1, -1),
                          c0, n0, m0, nseq, seq, gr["ctx"])
        if gr["ctx"]:
            hn, c_new, n_new, m_new = res
            outs["C"] = c_new
            outs["n"] = n_new.reshape(nseq, 1, 2, M_HEADS, M_DQK)
            outs["m"] = m_new.reshape(nseq, 1, 2, M_HEADS)
        else:
            (hn,) = res
        gr["x"] = _mlstm_out(gr["x"], mods, cmap(gr, PROJ_TB), hn, p_main,
                             mlstm_head_g[0].reshape(1, -1), w_mout)

    def peer(layer, final):
        wq_hi, wq_lo = _hi_lo(peer_w_q[layer])
        sk_hi, sk_lo = _hi_lo(peer_subkeys[layer].reshape(2 * P_HEADS, P_NKEYS, P_DKEY // 2))
        for gr in groups:
            mods = mods_all[layer]
            ht, thr, e1, s2 = _peer_project(gr["x"], mods, cmap(gr, PEER_TB), norm_g[layer, 1].reshape(1, d),
                                            wq_hi, wq_lo, sk_hi, sk_lo)
            gr["x"] = _peer_dense(gr["x"], mods, cmap(gr, PEER_TD), ht, thr, e1, s2, u_all, vt_all, layer, fg, final)

    u_all = peer_u.astype(BF16)
    vt_all = jnp.swapaxes(peer_v.astype(BF16), 1, 2)
    peer(0, False)

    wqkv = attn_w_qkv[0]
    nq = A_HEADS * A_HEAD_DIM
    nkv = A_KV_HEADS * A_HEAD_DIM
    dup = lambda w: jnp.tile(w.reshape(d, A_KV_HEADS, 1, A_HEAD_DIM), (1, 1, 2, 1)).reshape(d, 2 * nkv)
    w_qkv2 = jnp.concatenate([wqkv[:, :nq], dup(wqkv[:, nq:nq + nkv]), dup(wqkv[:, nq + nkv:])], axis=1).astype(BF16)
    li = jnp.arange(LANES)
    gsum = (li[:, None] // A_HEAD_DIM == li[None, :] // A_HEAD_DIM).astype(BF16)
    qg = jnp.tile(attn_q_g[0], LANES // A_HEAD_DIM).reshape(1, LANES)
    kg = jnp.tile(attn_k_g[0], LANES // A_HEAD_DIM).reshape(1, LANES)
    w_aout = attn_w_out[0].astype(BF16)
    for gr in groups:
        nseq, seq = gr["nseq"], gr["seq"]
        mods = mods_all[1]
        tabs = None if gr["ctx"] else _rope_tables(seq)
        res = _attn_project(gr["x"], mods, cmap(gr, PROJ_TB), norm_g[1, 0].reshape(1, d), w_qkv2, gsum, qg, kg,
                            tabs, nseq, seq, gr["ctx"])
        if gr["ctx"]:
            q, k2, v2, kc_new, vc_new = res
            outs["k"], outs["v"] = kc_new, vc_new
            cache = None
        else:
            q, k2, v2 = res
            cache = (jnp.tile(cache_attn_k[:, 0].astype(F32), (1, 1, 1, 2)),
                     jnp.tile(cache_attn_v[:, 0].astype(F32), (1, 1, 1, 2)))
        att = _attention(q, k2, v2, cache, nseq, seq)
        gr["x"] = _attn_out(gr["x"], mods, cmap(gr, PROJ_TB), att, w_aout)

    peer(1, True)

    y_prompt = groups[0]["x"].reshape(nb, seq_c, d)
    y_sample = groups[1]["x"].reshape(ndb, seq_s, d)
    return (y_prompt, y_sample, outs["C"], outs["n"], outs["m"], outs["k"], outs["v"])
```

```python
import functools

import jax
import jax.numpy as jnp
from jax import lax
from jax.experimental import pallas as pl
from jax.experimental.pallas import tpu as pltpu

F32 = jnp.float32
BF16 = jnp.bfloat16

EPS = 1e-6
LOG2E = 1.4426950408889634
D_MODEL = 1024
GRID_W = 64
ROPE_THETA = 10000.0

M_HEADS = 4
M_DQK = 128
M_DV = 256
M_CHUNK = 128
M_QK_DIM = 2 * M_HEADS * M_DQK
M_V_DIM = M_HEADS * M_DV
M_MAIN_DIM = M_QK_DIM + 2 * M_V_DIM

A_HEADS = 16
A_KV_HEADS = 4
A_HEAD_DIM = 64
A_ROPE_AXIS = A_HEAD_DIM // 2

P_HEADS = 8
P_NKEYS = 128
P_DKEY = 256
P_TOPK = 16

LANES = 128
PROJ_TB = 256
PEER_TB = 256
PEER_TD = 512
PEER_EC = 2048
ATTN_QB = 512
SCAN_HEADS = 2
VMEM_LIMIT = 56 * 1024 * 1024


def _cparams(*sem):
    return pltpu.CompilerParams(dimension_semantics=sem, vmem_limit_bytes=VMEM_LIMIT)


def _dg(a, b, ca=1, cb=0):
    return lax.dot_general(a, b, (((ca,), (cb,)), ((), ())), preferred_element_type=F32)


def _split2(x):
    hi = x.astype(BF16)
    lo = (x - hi.astype(F32)).astype(BF16)
    return hi, lo


def _mm3(a, b, ca=1, cb=0):
    ah, al = _split2(a)
    bh, bl = _split2(b)
    return _dg(ah, bh, ca, cb) + (_dg(ah, bl, ca, cb) + _dg(al, bh, ca, cb))


def _modulate(x, g, shift, scale):
    y = x * lax.rsqrt(jnp.mean(x * x, axis=-1, keepdims=True) + EPS)
    return (y * g) * (1.0 + scale) + shift


def _cond_map(base, per):
    if per is None:
        return lambda i, *_: (base, 0, 0)
    return lambda i, *_: (base + i // per, 0, 0)


def _ada_kernel(c_ref, w_ref, b_ref, o_ref):
    c = c_ref[...]
    a = c * jax.nn.sigmoid(c)
    o_ref[0] = _mm3(a, w_ref[0]) + b_ref[0]


def _ada_mods(cond, ada_w, ada_b):
    depth, d, n = ada_w.shape
    tn = 512
    rows = cond.shape[0]
    out = pl.pallas_call(
        _ada_kernel,
        grid=(depth, n // tn),
        in_specs=[
            pl.BlockSpec((rows, d), lambda l, j: (0, 0)),
            pl.BlockSpec((1, d, tn), lambda l, j: (l, 0, j)),
            pl.BlockSpec((1, 1, tn), lambda l, j: (l, 0, j)),
        ],
        out_specs=pl.BlockSpec((1, rows, tn), lambda l, j: (l, 0, j)),
        out_shape=jax.ShapeDtypeStruct((depth, rows, n), F32),
        compiler_params=_cparams("arbitrary", "arbitrary"),
        name="ada_mods",
    )(cond, ada_w, ada_b.reshape(depth, 1, n))
    return out.reshape(depth, rows, 6, d)


def _mproj_kernel(x_ref, m_ref, g_ref, w_ref, wgh_ref, wgl_ref, gb_ref, p_ref, gt_ref):
    h = _modulate(x_ref[...], g_ref[...], m_ref[0, 0:1, :], m_ref[0, 1:2, :])
    p_ref[...] = _dg(h.astype(BF16), w_ref[...])
    hh, hl = _split2(h)
    wh = wgh_ref[...]
    gt = _dg(wh, hh, 1, 1) + (_dg(wh, hl, 1, 1) + _dg(wgl_ref[...], hh, 1, 1))
    gt_ref[...] = gt + gb_ref[...]


def _mlstm_project(x, mods, cmap, g, w_main, wg_hi, wg_lo, gate_b):
    ntok = x.shape[0]
    tb = PROJ_TB
    ng = wg_hi.shape[0]
    return pl.pallas_call(
        _mproj_kernel,
        grid=(ntok // tb,),
        in_specs=[
            pl.BlockSpec((tb, D_MODEL), lambda i: (i, 0)),
            pl.BlockSpec((1, 6, D_MODEL), cmap),
            pl.BlockSpec((1, D_MODEL), lambda i: (0, 0)),
            pl.BlockSpec((D_MODEL, M_MAIN_DIM), lambda i: (0, 0)),
            pl.BlockSpec((ng, D_MODEL), lambda i: (0, 0)),
            pl.BlockSpec((ng, D_MODEL), lambda i: (0, 0)),
            pl.BlockSpec((ng, 1), lambda i: (0, 0)),
        ],
        out_specs=[
            pl.BlockSpec((tb, M_MAIN_DIM), lambda i: (i, 0)),
            pl.BlockSpec((ng, tb), lambda i: (0, i)),
        ],
        out_shape=[
            jax.ShapeDtypeStruct((ntok, M_MAIN_DIM), F32),
            jax.ShapeDtypeStruct((ng, ntok), F32),
        ],
        compiler_params=_cparams("arbitrary"),
        name="mlstm_project",
    )(x, mods, g, w_main, wg_hi, wg_lo, gate_b)


def _scan_kernel(seq, state_out, qp_ref, kp_ref, v_ref, g_ref, cwq_ref, cwk_ref, cbq_ref, cbk_ref,
                 c0_ref, n0_ref, m0_ref, hn_ref, *rest):
    if state_out:
        co_ref, no_ref, mo_ref, q_s, k_s, hf_s, hb_s, cf_s, cb_s = rest
    else:
        q_s, k_s, hf_s, hb_s, cf_s, cb_s = rest
    h_dir = (hf_s, hb_s)
    c_dir = (cf_s, cb_s)
    nc = seq // M_CHUNK
    nh = cf_s.shape[0]
    rows = lax.broadcasted_iota(jnp.int32, (seq, 1), 0)

    def conv(x, w_ref, b_ref):
        xm1 = jnp.where(rows == 0, 0.0, pltpu.roll(x, 1, 0))
        xp1 = jnp.where(rows == seq - 1, 0.0, pltpu.roll(x, seq - 1, 0))
        y = xm1 * w_ref[0:1, :] + x * w_ref[1:2, :] + xp1 * w_ref[2:3, :] + b_ref[...]
        return y * jax.nn.sigmoid(y)

    q_s[...] = conv(qp_ref[...], cwq_ref, cbq_ref)
    k_s[...] = conv(kp_ref[...], cwk_ref, cbk_ref) * (M_DQK ** -0.5)

    ti = lax.broadcasted_iota(jnp.int32, (M_CHUNK, M_CHUNK), 0)
    si = lax.broadcasted_iota(jnp.int32, (M_CHUNK, M_CHUNK), 1)
    eye = ti == si

    def row2col(r):
        return jnp.sum(jnp.where(eye, r, 0.0), axis=1, keepdims=True)

    def col2row(c):
        return jnp.sum(jnp.where(eye, c, 0.0), axis=0, keepdims=True)

    def chunk_step(dirn, hd, c, n, m):
        mask = (si <= ti) if dirn == 0 else (si >= ti)
        c_s = c_dir[dirn]
        qk_cols = slice(hd * M_DQK, (hd + 1) * M_DQK)
        v_cols = slice(hd * M_DV, (hd + 1) * M_DV)
        r0 = pl.multiple_of(c * M_CHUNK, M_CHUNK)
        g4 = g_ref[hd, c]
        ig = g4[2 * dirn:2 * dirn + 1, :]
        fp = g4[2 * dirn + 1:2 * dirn + 2, :]
        lf = jnp.minimum(fp, 0.0) - jnp.log1p(jnp.exp(-jnp.abs(fp)))
        b_col = jnp.sum(jnp.where(mask, lf, 0.0), axis=1, keepdims=True)
        b_row = col2row(b_col)
        b_end = jnp.sum(lf, axis=1, keepdims=True)
        qc = q_s[pl.ds(r0, M_CHUNK), qk_cols]
        kc = k_s[pl.ds(r0, M_CHUNK), qk_cols]
        vc = v_ref[pl.ds(r0, M_CHUNK), v_cols]
        d = jnp.where(mask, b_col - b_row + ig, -jnp.inf)
        inter = b_col + m
        m_t = jnp.maximum(inter, jnp.max(d, axis=1, keepdims=True))
        qb = qc.astype(BF16)
        w = jnp.exp(d - m_t) * _dg(qb, kc.astype(BF16), 1, 1)
        a = jnp.exp(inter - m_t)
        c_old = c_s[hd]
        num = a * _dg(qb, c_old.astype(BF16)) + _dg(w.astype(BF16), vc.astype(BF16))
        den = a * jnp.sum(qc * n, axis=1, keepdims=True) + jnp.sum(w, axis=1, keepdims=True)
        h_dir[dirn][pl.ds(r0, M_CHUNK), v_cols] = num / jnp.maximum(jnp.abs(den), jnp.exp(-m_t))
        to_end = b_end - b_row + ig
        m_new = jnp.maximum(b_end + m, jnp.max(to_end, axis=1, keepdims=True))
        decay = jnp.exp(b_end + m - m_new)
        kw = kc * row2col(jnp.exp(to_end - m_new))
        c_s[hd] = decay * c_old + _mm3(kw.T, vc)
        n_new = decay * n + jnp.sum(kw, axis=0, keepdims=True)
        return n_new, m_new

    chains = [(dirn, hd) for hd in range(nh) for dirn in range(2)]
    for dirn, hd in chains:
        c_dir[dirn][hd] = c0_ref[0, 0, dirn, hd]

    def body(it, carry):
        out = []
        for i, (dirn, hd) in enumerate(chains):
            c = it if dirn == 0 else nc - 1 - it
            out.extend(chunk_step(dirn, hd, c, carry[2 * i], carry[2 * i + 1]))
        return tuple(out)

    init = []
    for dirn, hd in chains:
        init.extend((n0_ref[0, 0, dirn, hd], m0_ref[0, 0, dirn, hd]))
    fin = lax.fori_loop(0, nc, body, tuple(init))
    if state_out:
        for i, (dirn, hd) in enumerate(chains):
            co_ref[0, 0, dirn, hd] = c_dir[dirn][hd]
            no_ref[0, 0, dirn, hd] = fin[2 * i]
            mo_ref[0, 0, dirn, hd] = fin[2 * i + 1]

    for hd in range(nh):
        v_cols = slice(hd * M_DV, (hd + 1) * M_DV)
        hm = hf_s[:, v_cols] + hb_s[:, v_cols]
        hn_ref[:, v_cols] = hm * lax.rsqrt(jnp.mean(hm * hm, axis=-1, keepdims=True) + EPS)


def _mlstm_scan(p_main, gates, conv_w, conv_b, c0, n0, m0, nseq, seq, state_out):
    ntok = nseq * seq
    nc = seq // M_CHUNK
    hps = SCAN_HEADS
    ngrp = M_HEADS // hps
    qblk = hps * M_DQK
    vblk = hps * M_DV
    st5 = lambda b, h: (b, 0, 0, h, 0, 0)
    in_specs = [
        pl.BlockSpec((seq, qblk), lambda b, h: (b, h)),
        pl.BlockSpec((seq, qblk), lambda b, h: (b, ngrp + h)),
        pl.BlockSpec((seq, vblk), lambda b, h: (b, M_QK_DIM // vblk + h)),
        pl.BlockSpec((hps, nc, 4, M_CHUNK), lambda b, h: (h, b, 0, 0)),
        pl.BlockSpec((3, qblk), lambda b, h: (0, h)),
        pl.BlockSpec((3, qblk), lambda b, h: (0, ngrp + h)),
        pl.BlockSpec((1, qblk), lambda b, h: (0, h)),
        pl.BlockSpec((1, qblk), lambda b, h: (0, ngrp + h)),
        pl.BlockSpec((1, 1, 2, hps, M_DQK, M_DV), st5),
        pl.BlockSpec((1, 1, 2, hps, 1, M_DQK), st5),
        pl.BlockSpec((1, 1, 2, hps, 1, 1), st5),
    ]
    out_specs = [pl.BlockSpec((seq, vblk), lambda b, h: (b, h))]
    out_shape = [jax.ShapeDtypeStruct((ntok, M_V_DIM), F32)]
    if state_out:
        out_specs += [
            pl.BlockSpec((1, 1, 2, hps, M_DQK, M_DV), st5),
            pl.BlockSpec((1, 1, 2, hps, 1, M_DQK), st5),
            pl.BlockSpec((1, 1, 2, hps, 1, 1), st5),
        ]
        out_shape += [
            jax.ShapeDtypeStruct((nseq, 1, 2, M_HEADS, M_DQK, M_DV), F32),
            jax.ShapeDtypeStruct((nseq, 1, 2, M_HEADS, 1, M_DQK), F32),
            jax.ShapeDtypeStruct((nseq, 1, 2, M_HEADS, 1, 1), F32),
        ]
    return pl.pallas_call(
        functools.partial(_scan_kernel, seq, state_out),
        grid=(nseq, ngrp),
        in_specs=in_specs,
        out_specs=out_specs,
        out_shape=out_shape,
        scratch_shapes=[
            pltpu.VMEM((seq, qblk), F32),
            pltpu.VMEM((seq, qblk), F32),
            pltpu.VMEM((seq, vblk), F32),
            pltpu.VMEM((seq, vblk), F32),
            pltpu.VMEM((hps, M_DQK, M_DV), F32),
            pltpu.VMEM((hps, M_DQK, M_DV), F32),
        ],
        compiler_params=_cparams("arbitrary", "arbitrary"),
        name="mlstm_scan_ctx" if state_out else "mlstm_scan_smp",
    )(p_main, p_main, p_main, gates, conv_w, conv_w, conv_b, conv_b, c0, n0, m0)


def _mout_kernel(x_ref, m_ref, hn_ref, o_ref, hg_ref, w_ref, y_ref):
    t = hn_ref[...] * hg_ref[...] * jax.nn.sigmoid(o_ref[...])
    y_ref[...] = x_ref[...] + m_ref[0, 2:3, :] * _dg(t.astype(BF16), w_ref[...])


def _mlstm_out(x, mods, cmap, hn, p_main, head_g, w_out):
    ntok = x.shape[0]
    tb = PROJ_TB
    return pl.pallas_call(
        _mout_kernel,
        grid=(ntok // tb,),
        in_specs=[
            pl.BlockSpec((tb, D_MODEL), lambda i: (i, 0)),
            pl.BlockSpec((1, 6, D_MODEL), cmap),
            pl.BlockSpec((tb, M_V_DIM), lambda i: (i, 0)),
            pl.BlockSpec((tb, M_V_DIM), lambda i: (i, (M_QK_DIM + M_V_DIM) // M_V_DIM)),
            pl.BlockSpec((1, M_V_DIM), lambda i: (0, 0)),
            pl.BlockSpec((M_V_DIM, D_MODEL), lambda i: (0, 0)),
        ],
        out_specs=pl.BlockSpec((tb, D_MODEL), lambda i: (i, 0)),
        out_shape=jax.ShapeDtypeStruct((ntok, D_MODEL), F32),
        compiler_params=_cparams("arbitrary"),
        name="mlstm_out",
    )(x, mods, hn, p_main, head_g, w_out)


def _aout_kernel(x_ref, m_ref, a_ref, w_ref, y_ref):
    y_ref[...] = x_ref[...] + m_ref[0, 2:3, :] * _dg(a_ref[...].astype(BF16), w_ref[...])


def _attn_out(x, mods, cmap, att, w_out):
    ntok = x.shape[0]
    tb = PROJ_TB
    return pl.pallas_call(
        _aout_kernel,
        grid=(ntok // tb,),
        in_specs=[
            pl.BlockSpec((tb, D_MODEL), lambda i: (i, 0)),
            pl.BlockSpec((1, 6, D_MODEL), cmap),
            pl.BlockSpec((tb, D_MODEL), lambda i: (i, 0)),
            pl.BlockSpec((D_MODEL, D_MODEL), lambda i: (0, 0)),
        ],
        out_specs=pl.BlockSpec((tb, D_MODEL), lambda i: (i, 0)),
        out_shape=jax.ShapeDtypeStruct((ntok, D_MODEL), F32),
        compiler_params=_cparams("arbitrary"),
        name="attn_out",
    )(x, mods, att, w_out)


def _aproj_kernel(rope, cache_out, x_ref, m_ref, g_ref, w_ref, gs_ref, qg_ref, kg_ref, *rest):
    rest = list(rest)
    if rope:
        cos_ref, sin_ref = rest[:2]
        rest = rest[2:]
    q_ref, k_ref, v_ref = rest[:3]
    rest = rest[3:]
    h = _modulate(x_ref[...], g_ref[...], m_ref[0, 0:1, :], m_ref[0, 1:2, :])
    p = _dg(h.astype(BF16), w_ref[...])
    nq = A_HEADS * A_HEAD_DIM
    nk = 2 * A_KV_HEADS * A_HEAD_DIM
    gs = gs_ref[...]
    lane = lax.broadcasted_iota(jnp.int32, (1, LANES), 1)
    first_half = (lane % A_ROPE_AXIS) < (A_ROPE_AXIS // 2)

    def norm_rope(xs, gain):
        sq = xs * xs
        s1 = sq.astype(BF16)
        r1 = sq - s1.astype(F32)
        s2 = r1.astype(BF16)
        s3 = (r1 - s2.astype(F32)).astype(BF16)
        ss = _dg(s1, gs) + (_dg(s2, gs) + _dg(s3, gs))
        y = xs * lax.rsqrt(ss * (1.0 / A_HEAD_DIM) + EPS) * gain
        if rope:
            half = A_ROPE_AXIS // 2
            swapped = jnp.where(first_half, pltpu.roll(y, LANES - half, 1), pltpu.roll(y, half, 1))
            y = y * cos_ref[...] + swapped * sin_ref[...]
        return y

    for s in range(nq // LANES):
        q_ref[:, s * LANES:(s + 1) * LANES] = norm_rope(p[:, s * LANES:(s + 1) * LANES], qg_ref[...])
    for s in range(nk // LANES):
        kn = norm_rope(p[:, nq + s * LANES:nq + (s + 1) * LANES], kg_ref[...])
        k_ref[:, s * LANES:(s + 1) * LANES] = kn
        vs = p[:, nq + nk + s * LANES:nq + nk + (s + 1) * LANES]
        v_ref[:, s * LANES:(s + 1) * LANES] = vs
        if cache_out:
            kc_ref, vc_ref = rest
            kc_ref[0, 0, s] = kn[:, :A_HEAD_DIM]
            vc_ref[0, 0, s] = vs[:, :A_HEAD_DIM]


def _attn_project(x, mods, cmap, g, w_qkv2, gsum, qg, kg, rope_tabs, nseq, seq, cache_out):
    ntok = x.shape[0]
    tb = PROJ_TB
    nq = A_HEADS * A_HEAD_DIM
    nk = 2 * A_KV_HEADS * A_HEAD_DIM
    per = seq // tb
    rope = rope_tabs is not None
    in_specs = [
        pl.BlockSpec((tb, D_MODEL), lambda i: (i, 0)),
        pl.BlockSpec((1, 6, D_MODEL), cmap),
        pl.BlockSpec((1, D_MODEL), lambda i: (0, 0)),
        pl.BlockSpec((D_MODEL, nq + 2 * nk), lambda i: (0, 0)),
        pl.BlockSpec((LANES, LANES), lambda i: (0, 0)),
        pl.BlockSpec((1, LANES), lambda i: (0, 0)),
        pl.BlockSpec((1, LANES), lambda i: (0, 0)),
    ]
    args = [x, mods, g, w_qkv2, gsum, qg, kg]
    if rope:
        in_specs += [pl.BlockSpec((tb, LANES), lambda i: (i % per, 0))] * 2
        args += list(rope_tabs)
    out_specs = [
        pl.BlockSpec((tb, nq), lambda i: (i, 0)),
        pl.BlockSpec((tb, nk), lambda i: (i, 0)),
        pl.BlockSpec((tb, nk), lambda i: (i, 0)),
    ]
    out_shape = [
        jax.ShapeDtypeStruct((ntok, nq), F32),
        jax.ShapeDtypeStruct((ntok, nk), F32),
        jax.ShapeDtypeStruct((ntok, nk), F32),
    ]
    if cache_out:
        assert seq == tb
        cspec = pl.BlockSpec((1, 1, A_KV_HEADS, seq, A_HEAD_DIM), lambda i: (i, 0, 0, 0, 0))
        cshape = jax.ShapeDtypeStruct((nseq, 1, A_KV_HEADS, seq, A_HEAD_DIM), F32)
        out_specs += [cspec, cspec]
        out_shape += [cshape, cshape]
    return pl.pallas_call(
        functools.partial(_aproj_kernel, rope, cache_out),
        grid=(ntok // tb,),
        in_specs=in_specs,
        out_specs=out_specs,
        out_shape=out_shape,
        compiler_params=_cparams("arbitrary"),
        name="attn_project_ctx" if cache_out else "attn_project_smp",
    )(*args)


def _attn_kernel(cached, q_ref, k_ref, v_ref, *rest):
    if cached:
        kc_ref, vc_ref, o_ref = rest
    else:
        (o_ref,) = rest
    lane = lax.broadcasted_iota(jnp.int32, (1, LANES), 1)
    q = q_ref[...] * (A_HEAD_DIM ** -0.5)
    k = k_ref[...].astype(BF16)
    v = v_ref[...]
    if cached:
        kc = kc_ref[0, 0].astype(BF16)
        vc = vc_ref[0, 0]
    acc = jnp.zeros(q.shape, F32)
    for e in range(2):
        sel = (lane < A_HEAD_DIM) if e == 0 else (lane >= A_HEAD_DIM)
        qe = jnp.where(sel, q, 0.0).astype(BF16)
        s = _dg(qe, k, 1, 1)
        mx = jnp.max(s, axis=1, keepdims=True)
        if cached:
            sc = _dg(qe, kc, 1, 1)
            mx = jnp.maximum(mx, jnp.max(sc, axis=1, keepdims=True))
        p = jnp.exp(s - mx)
        l = jnp.sum(p, axis=1, keepdims=True)
        o = _dg(p.astype(BF16), jnp.where(sel, v, 0.0).astype(BF16))
        if cached:
            pc = jnp.exp(sc - mx)
            l = l + jnp.sum(pc, axis=1, keepdims=True)
            o = o + _dg(pc.astype(BF16), jnp.where(sel, vc, 0.0).astype(BF16))
        acc = acc + o / l
    o_ref[...] = acc


def _attention(q, k2, v2, cache, nseq, seq):
    ntok = q.shape[0]
    qb = min(ATTN_QB, seq)
    nqb = seq // qb
    npair = A_HEADS // 2
    in_specs = [
        pl.BlockSpec((qb, LANES), lambda b, hp, j: (b * nqb + j, hp)),
        pl.BlockSpec((seq, LANES), lambda b, hp, j: (b, hp // 2)),
        pl.BlockSpec((seq, LANES), lambda b, hp, j: (b, hp // 2)),
    ]
    args = [q, k2, v2]
    if cache is not None:
        past = cache[0].shape[2]
        cspec = pl.BlockSpec((1, 1, past, LANES), lambda b, hp, j: (b, hp // 2, 0, 0))
        in_specs += [cspec, cspec]
        args += list(cache)
    return pl.pallas_call(
        functools.partial(_attn_kernel, cache is not None),
        grid=(nseq, npair, nqb),
        in_specs=in_specs,
        out_specs=pl.BlockSpec((qb, LANES), lambda b, hp, j: (b * nqb + j, hp)),
        out_shape=jax.ShapeDtypeStruct((ntok, A_HEADS * A_HEAD_DIM), F32),
        compiler_params=_cparams("arbitrary", "arbitrary", "arbitrary"),
        name="attention_smp" if cache is not None else "attention_ctx",
    )(*args)


def _batcher_pairs(n):
    pairs = []
    p = 1
    while p < n:
        k = p
        while k >= 1:
            for j in range(k % p, n - k, 2 * k):
                for i in range(min(k, n - j - k)):
                    if (i + j) // (2 * p) == (i + j + k) // (2 * p):
                        pairs.append((i + j, i + j + k))
            k //= 2
        p *= 2
    return pairs


def _sort_levels(levels):
    lv = list(levels)
    for i, j in _batcher_pairs(len(lv)):
        lv[i], lv[j] = jnp.maximum(lv[i], lv[j]), jnp.minimum(lv[i], lv[j])
    return lv


def _pop_top(levels, k, sub):
    lv = list(levels)
    n = len(lv)
    outs = []
    for it in range(k):
        head = lv[0]
        m = jnp.max(head, axis=0, keepdims=True)
        outs.append(m)
        rem = k - 1 - it
        if rem == 0:
            break
        first = jnp.min(jnp.where(head == m, sub, 8.0), axis=0, keepdims=True)
        pop = sub == first
        for q in range(min(n, rem)):
            nxt = lv[q + 1] if q + 1 < n else -jnp.inf
            lv[q] = jnp.where(pop, nxt, lv[q])
    return outs


def _cand_levels(a, b, sub):
    k1 = P_TOPK + 1
    acol = jnp.where(sub == 0.0, a[0], jnp.where(sub == 1.0, a[1], jnp.where(sub == 2.0, a[2], a[3])))
    bcol = jnp.where(sub == 4.0, b[0], jnp.where(sub == 5.0, b[1], b[2]))
    low = sub < 4.0
    levels = []
    for lvl in range(k1):
        na = min(4, k1 // (lvl + 1))
        nb = min(3, k1 // (lvl + 5)) if lvl + 4 < k1 else 0
        valid = sub < float(na)
        x = acol + b[lvl]
        if nb:
            x = jnp.where(low, x, a[lvl + 4] + bcol)
            valid = valid | ((sub >= 4.0) & (sub < float(4 + nb)))
        levels.append(jnp.where(valid, x, -jnp.inf))
    return levels


def _pproj_kernel(x_ref, m_ref, g_ref, wq_ref, skh_ref, skl_ref,
                  ht_ref, thr_ref, e1_ref, s2_ref):
    h = _modulate(x_ref[...], g_ref[...], m_ref[0, 3:4, :], m_ref[0, 4:5, :])
    ht_ref[...] = h.T.astype(BF16)
    q = _dg(h.astype(BF16), wq_ref[...])
    k1 = P_TOPK + 1
    sub = lax.broadcasted_iota(jnp.int32, (8, LANES), 0).astype(F32)
    for p in range(P_HEADS):
        sc = []
        for hf in range(2):
            ph = 2 * p + hf
            qh, ql = _split2(q[:, ph * LANES:(ph + 1) * LANES])
            kh = skh_ref[ph]
            sc.append(_dg(kh, qh, 1, 1) + (_dg(kh, ql, 1, 1) + _dg(skl_ref[ph], qh, 1, 1)))
        for lt in range(q.shape[0] // LANES):
            ls = slice(lt * LANES, (lt + 1) * LANES)
            s1 = sc[0][:, ls]
            s2 = sc[1][:, ls]
            a = _pop_top(_sort_levels([s1[8 * r:8 * r + 8] for r in range(P_NKEYS // 8)]), k1, sub)
            b = _pop_top(_sort_levels([s2[8 * r:8 * r + 8] for r in range(P_NKEYS // 8)]), k1, sub)
            v = _pop_top(_cand_levels(a, b, sub), k1, sub)
            tau = 0.5 * (v[P_TOPK - 1] + v[P_TOPK])
            z = jnp.ones_like(tau)
            for kk in range(1, P_TOPK):
                z = z + jnp.exp(v[kk] - v[0])
            thr_ref[p, :, ls] = ((tau - b[0]) - s1) * LOG2E
            e1_ref[p, :, ls] = jnp.exp(s1 - a[0]) * (0.5 / z)
            s2_ref[p, :, ls] = (s2 - b[0]) * LOG2E


def _peer_project(x, mods, cmap, g, wq, sk_hi, sk_lo):
    ntok = x.shape[0]
    tb = PEER_TB
    nq = P_HEADS * P_DKEY
    sel_spec = pl.BlockSpec((P_HEADS, P_NKEYS, tb), lambda i: (0, 0, i))
    sel_shape = jax.ShapeDtypeStruct((P_HEADS, P_NKEYS, ntok), F32)
    return pl.pallas_call(
        _pproj_kernel,
        grid=(ntok // tb,),
        in_specs=[
            pl.BlockSpec((tb, D_MODEL), lambda i: (i, 0)),
            pl.BlockSpec((1, 6, D_MODEL), cmap),
            pl.BlockSpec((1, D_MODEL), lambda i: (0, 0)),
            pl.BlockSpec((D_MODEL, nq), lambda i: (0, 0)),
            pl.BlockSpec((2 * P_HEADS, P_NKEYS, P_DKEY // 2), lambda i: (0, 0, 0)),
            pl.BlockSpec((2 * P_HEADS, P_NKEYS, P_DKEY // 2), lambda i: (0, 0, 0)),
        ],
        out_specs=[pl.BlockSpec((D_MODEL, tb), lambda i: (0, i)), sel_spec, sel_spec, sel_spec],
        out_shape=[jax.ShapeDtypeStruct((D_MODEL, ntok), BF16), sel_shape, sel_shape, sel_shape],
        compiler_params=_cparams("arbitrary"),
        name="peer_project",
    )(x, mods, g, wq, sk_hi, sk_lo)


def _pdense_kernel(final, ne, x_ref, m_ref, ht_ref, thr_ref, e1_ref, s2_ref, u_ref, vt_ref, fg_ref,
                   y_ref, act0_s, act1_s, gt0_s, gt1_s, acc_s):
    s = pl.program_id(0)
    nchunk = pl.num_programs(0) - 2
    jc = jnp.clip(s - 2, 0, nchunk - 1) % ne

    @pl.when(s == 0)
    def _():
        act1_s[...] = jnp.zeros(act1_s.shape, F32)
        gt1_s[...] = jnp.zeros(gt1_s.shape, BF16)
        acc_s[...] = jnp.zeros(acc_s.shape, F32)

    @pl.when(s % 2 == 0)
    def _():
        _pdense_step(u_ref, ht_ref, thr_ref, e1_ref, s2_ref, vt_ref, act0_s, act1_s, gt0_s, gt1_s, acc_s)

    @pl.when(s % 2 == 1)
    def _():
        _pdense_step(u_ref, ht_ref, thr_ref, e1_ref, s2_ref, vt_ref, act1_s, act0_s, gt1_s, gt0_s, acc_s)

    @pl.when((s >= 2) & (jc == ne - 1))
    def _():
        xn = x_ref[...] + m_ref[0, 5:6, :] * acc_s[...].T
        if final:
            xn = xn * lax.rsqrt(jnp.mean(xn * xn, axis=-1, keepdims=True) + EPS) * fg_ref[...]
        y_ref[...] = xn
        acc_s[...] = jnp.zeros(acc_s.shape, F32)


def _pdense_step(u_ref, ht_ref, thr_ref, e1_ref, s2_ref, vt_ref, act_w, act_r, gt_w, gt_r, acc_s):
    ec, td = act_w.shape
    na, nb = 4, 32
    half = td // 2

    def half_body(hh, carry):
        h0 = hh * half
        hs = pl.ds(pl.multiple_of(h0, half), half)
        act_w[:, hs] = _dg(u_ref[...], ht_ref[:, hs])
        acc_s[:, hs] += _dg(vt_ref[...], gt_r[:, hs])
        for lt in range(half // LANES):
            ls = pl.ds(pl.multiple_of(h0 + lt * LANES, LANES), LANES)
            for aq in range(ec // P_NKEYS // na):
                for bq in range(P_NKEYS // nb):
                    bs = slice(bq * nb, (bq + 1) * nb)
                    w = [jnp.zeros((nb, LANES), F32) for _ in range(na)]
                    for p in range(P_HEADS):
                        s2 = s2_ref[p, bs, ls]
                        e2 = jnp.exp2(s2)
                        for ai in range(na):
                            al = aq * na + ai
                            sel = jnp.where(s2 >= thr_ref[p, al:al + 1, ls], e2, 0.0)
                            w[ai] = w[ai] + sel * e1_ref[p, al:al + 1, ls]
                    for ai in range(na):
                        r0 = (aq * na + ai) * P_NKEYS + bq * nb
                        xa = act_r[r0:r0 + nb, ls]
                        gel2 = xa * (1.0 + lax.erf(xa * (2.0 ** -0.5)))
                        gt_w[r0:r0 + nb, ls] = (w[ai] * gel2).astype(BF16)
        return carry

    lax.fori_loop(0, td // half, half_body, 0)


def _peer_dense(x, mods, cmap, ht, thr, e1, s2, u, vt, layer, final_g, final):
    ntok = x.shape[0]
    td = PEER_TD
    ec = PEER_EC
    ne = u.shape[1] // ec
    nchunk = (ntok // td) * ne
    ca = lambda s: jnp.minimum(s, nchunk - 1)
    cb = lambda s: jnp.clip(s - 1, 0, nchunk - 1)
    cc = lambda s: jnp.clip(s - 2, 0, nchunk - 1)
    sel_spec = pl.BlockSpec((P_HEADS, P_NKEYS, td), lambda s: (0, 0, cb(s) // ne))
    row_spec = pl.BlockSpec((P_HEADS, ec // P_NKEYS, td), lambda s: (0, cb(s) % ne, cb(s) // ne))
    return pl.pallas_call(
        functools.partial(_pdense_kernel, final, ne),
        grid=(nchunk + 2,),
        in_specs=[
            pl.BlockSpec((td, D_MODEL), lambda s: (cc(s) // ne, 0)),
            pl.BlockSpec((1, 6, D_MODEL), lambda s: cmap(cc(s) // ne)),
            pl.BlockSpec((D_MODEL, td), lambda s: (0, ca(s) // ne)),
            row_spec, row_spec, sel_spec,
            pl.BlockSpec((None, ec, D_MODEL), lambda s: (layer, ca(s) % ne, 0)),
            pl.BlockSpec((None, D_MODEL, ec), lambda s: (layer, 0, cc(s) % ne)),
            pl.BlockSpec((1, D_MODEL), lambda s: (0, 0)),
        ],
        out_specs=pl.BlockSpec((td, D_MODEL), lambda s: (cc(s) // ne, 0)),
        out_shape=jax.ShapeDtypeStruct((ntok, D_MODEL), F32),
        scratch_shapes=[
            pltpu.VMEM((ec, td), F32),
            pltpu.VMEM((ec, td), F32),
            pltpu.VMEM((ec, td), BF16),
            pltpu.VMEM((ec, td), BF16),
            pltpu.VMEM((D_MODEL, td), F32),
        ],
        compiler_params=_cparams("arbitrary"),
        name="peer_dense",
    )(x, mods, ht, thr, e1, s2, u, vt, final_g)


def _rope_tables(seq):
    n_rows = seq // GRID_W
    rows = jnp.repeat(jnp.arange(n_rows, dtype=F32), GRID_W)
    cols = jnp.tile(jnp.arange(GRID_W, dtype=F32), n_rows)
    half = A_ROPE_AXIS // 2
    inv_freq = ROPE_THETA ** (-jnp.arange(half, dtype=F32) / half)
    ar = rows[:, None] * inv_freq
    ac = cols[:, None] * inv_freq
    cos = jnp.concatenate([jnp.cos(ar), jnp.cos(ar), jnp.cos(ac), jnp.cos(ac)], axis=-1)
    sin = jnp.concatenate([-jnp.sin(ar), jnp.sin(ar), -jnp.sin(ac), jnp.sin(ac)], axis=-1)
    return jnp.tile(cos, (1, LANES // A_HEAD_DIM)), jnp.tile(sin, (1, LANES // A_HEAD_DIM))


def _hi_lo(w):
    hi = w.astype(BF16)
    return hi, (w - hi.astype(F32)).astype(BF16)


def kernel(x_prompt, x_sample, state_mlstm_C, state_mlstm_n, state_mlstm_m, cache_attn_k, cache_attn_v, c, c_ctx, ada_w, ada_b, norm_g, final_g, mlstm_w_in, mlstm_conv_w, mlstm_conv_b, mlstm_gate_b, mlstm_head_g, mlstm_w_out, attn_w_qkv, attn_q_g, attn_k_g, attn_w_out, peer_w_q, peer_subkeys, peer_u, peer_v):
    nb, seq_c, d = x_prompt.shape
    ndb, seq_s, _ = x_sample.shape
    assert d == D_MODEL and seq_c % PROJ_TB == 0 and seq_s % PROJ_TB == 0
    assert (nb * seq_c) % PEER_TD == 0 and seq_s % PEER_TD == 0

    nrow = -(-(1 + ndb) // 8) * 8
    cond = jnp.concatenate([c_ctx[None, :], c, jnp.zeros((nrow - 1 - ndb, d), F32)], axis=0)
    mods_all = _ada_mods(cond, ada_w, ada_b)

    groups = [
        dict(x=x_prompt.reshape(nb * seq_c, d), nseq=nb, seq=seq_c, ctx=True),
        dict(x=x_sample.reshape(ndb * seq_s, d), nseq=ndb, seq=seq_s, ctx=False),
    ]

    def cmap(gr, tb):
        return _cond_map(0, None) if gr["ctx"] else _cond_map(1, gr["seq"] // tb)

    outs = {}
    fg = final_g.reshape(1, d)

    w_in = mlstm_w_in[0]
    w_main = w_in[:, :M_MAIN_DIM].astype(BF16)
    perm = jnp.arange(4 * M_HEADS).reshape(4, M_HEADS).T.reshape(-1)
    wg_t = w_in[:, M_MAIN_DIM:].T[perm]
    wg_hi, wg_lo = _hi_lo(wg_t)
    gate_b = mlstm_gate_b[0][perm].reshape(-1, 1)
    w_mout = mlstm_w_out[0].astype(BF16)
    for gr in groups:
        nseq, seq = gr["nseq"], gr["seq"]
        mods = mods_all[0]
        p_main, gt = _mlstm_project(gr["x"], mods, cmap(gr, PROJ_TB), norm_g[0, 0].reshape(1, d),
                                    w_main, wg_hi, wg_lo, gate_b)
        nc = seq // M_CHUNK
        gates = gt.reshape(M_HEADS, 4, nseq * nc, M_CHUNK).transpose(0, 2, 1, 3)
        if gr["ctx"]:
            c0 = jnp.zeros((nseq, 1, 2, M_HEADS, M_DQK, M_DV), F32)
            n0 = jnp.zeros((nseq, 1, 2, M_HEADS, 1, M_DQK), F32)
            m0 = jnp.zeros((nseq, 1, 2, M_HEADS, 1, 1), F32)
        else:
            c0 = state_mlstm_C.astype(F32)
            n0 = state_mlstm_n.astype(F32).reshape(nseq, -1, 2, M_HEADS, 1, M_DQK)
            m0 = state_mlstm_m.astype(F32).reshape(nseq, -1, 2, M_HEADS, 1, 1)
        res = _mlstm_scan(p_main, gates, mlstm_conv_w[0], mlstm_conv_b[0].reshape(1, -1),
                          c0, n0, m0, nseq, seq, gr["ctx"])
        if gr["ctx"]:
            hn, c_new, n_new, m_new = res
            outs["C"] = c_new
            outs["n"] = n_new.reshape(nseq, 1, 2, M_HEADS, M_DQK)
            outs["m"] = m_new.reshape(nseq, 1, 2, M_HEADS)
        else:
            (hn,) = res
        gr["x"] = _mlstm_out(gr["x"], mods, cmap(gr, PROJ_TB), hn, p_main,
                             mlstm_head_g[0].reshape(1, -1), w_mout)

    def peer(layer, final):
        wq = peer_w_q[layer].astype(BF16)
        sk_hi, sk_lo = _hi_lo(peer_subkeys[layer].reshape(2 * P_HEADS, P_NKEYS, P_DKEY // 2))
        for gr in groups:
            mods = mods_all[layer]
            ht, thr, e1, s2 = _peer_project(gr["x"], mods, cmap(gr, PEER_TB), norm_g[layer, 1].reshape(1, d),
                                            wq, sk_hi, sk_lo)
            gr["x"] = _peer_dense(gr["x"], mods, cmap(gr, PEER_TD), ht, thr, e1, s2, u_all, vt_all, layer, fg, final)

    u_all = peer_u.astype(BF16)
    vt_all = jnp.swapaxes(peer_v.astype(BF16), 1, 2)
    peer(0, False)

    wqkv = attn_w_qkv[0]
    nq = A_HEADS * A_HEAD_DIM
    nkv = A_KV_HEADS * A_HEAD_DIM
    dup = lambda w: jnp.tile(w.reshape(d, A_KV_HEADS, 1, A_HEAD_DIM), (1, 1, 2, 1)).reshape(d, 2 * nkv)
    w_qkv2 = jnp.concatenate([wqkv[:, :nq], dup(wqkv[:, nq:nq + nkv]), dup(wqkv[:, nq + nkv:])], axis=1).astype(BF16)
    li = jnp.arange(LANES)
    gsum = (li[:, None] // A_HEAD_DIM == li[None, :] // A_HEAD_DIM).astype(BF16)
    qg = jnp.tile(attn_q_g[0], LANES // A_HEAD_DIM).reshape(1, LANES)
    kg = jnp.tile(attn_k_g[0], LANES // A_HEAD_DIM).reshape(1, LANES)
    w_aout = attn_w_out[0].astype(BF16)
    for gr in groups:
        nseq, seq = gr["nseq"], gr["seq"]
        mods = mods_all[1]
        tabs = None if gr["ctx"] else _rope_tables(seq)
        res = _attn_project(gr["x"], mods, cmap(gr, PROJ_TB), norm_g[1, 0].reshape(1, d), w_qkv2, gsum, qg, kg,
                            tabs, nseq, seq, gr["ctx"])
        if gr["ctx"]:
            q, k2, v2, kc_new, vc_new = res
            outs["k"], outs["v"] = kc_new, vc_new
            cache = None
        else:
            q, k2, v2 = res
            cache = (jnp.tile(cache_attn_k[:, 0].astype(F32), (1, 1, 1, 2)),
                     jnp.tile(cache_attn_v[:, 0].astype(F32), (1, 1, 1, 2)))
        att = _attention(q, k2, v2, cache, nseq, seq)
        gr["x"] = _attn_out(gr["x"], mods, cmap(gr, PROJ_TB), att, w_aout)

    peer(1, True)

    y_prompt = groups[0]["x"].reshape(nb, seq_c, d)
    y_sample = groups[1]["x"].reshape(ndb, seq_s, d)
    return (y_prompt, y_sample, outs["C"], outs["n"], outs["m"], outs["k"], outs["v"])
```
